```python
import math
import jax
import jax.numpy as jnp
from jax import lax
import numpy as np

D_MODEL = 2048
BATCH = 4
SEQ = 2048
DEPTH = 2
DEC_BATCH = 128
DEC_SEQ = 4
PAST_LEN = 2048
PAGE_SIZE = 128

HEAD_DIM = 128
N_HEADS = D_MODEL // HEAD_DIM
DIFF_HEADS = N_HEADS // 2
DIFF_DK = HEAD_DIM // 2
NSA_HEADS = N_HEADS - DIFF_HEADS
NSA_KV_HEADS = 2
NSA_GROUP = NSA_HEADS // NSA_KV_HEADS
NSA_CMP_LEN = 32
NSA_CMP_STRIDE = 16
NSA_CMP_HIDDEN = 256
NSA_SEL_BLOCK = 64
NSA_SEL_TOPK = 16
NSA_WINDOW = 512
MOBA_HEADS = N_HEADS
MOBA_BLOCK = 256
MOBA_TOPK = 3
MOBA_Q_CHUNK = 16
Q_BLOCK = 128
REL_BUCKETS = 32
REL_MAX_DIST = 128
N_GROUPS = 4
EXPERTS_PER_GROUP = 4
N_EXPERTS = N_GROUPS * EXPERTS_PER_GROUP
EXPERT_TOPK = 2
D_EXPERT = D_MODEL // 4
DEEPNORM_ALPHA = (2 * DEPTH) ** 0.25
DEEPNORM_BETA = (8 * DEPTH) ** -0.25
LN_EPS = 1e-5
RMS_EPS = 1e-6
N_EVEN = (DEPTH + 1) // 2
N_ODD = DEPTH // 2
AB_SIZES = (DIFF_HEADS * 2 * DIFF_DK,) * 2 + (DIFF_HEADS * HEAD_DIM, NSA_HEADS * HEAD_DIM) + (NSA_KV_HEADS * HEAD_DIM,) * 6 + (NSA_HEADS * 3,)
AB_COLS = sum(AB_SIZES)
F32 = jnp.float32

kernel_name = 'hybrid_diff_nsa_moba_hmoe_step'


def layer_norm(x, g, b):
    xf = x.astype(F32)
    xc = xf - jnp.mean(xf, -1, keepdims=True)
    var = jnp.mean(xc * xc, -1, keepdims=True)
    return (xc * lax.rsqrt(var + LN_EPS) * g.astype(F32) + b.astype(F32)).astype(x.dtype)


def ada_in(x, c, w, b):
    m = (jax.nn.silu(c) @ w + b)[:, None, :]
    shift, scale, gate = jnp.split(m, 3, axis=-1)
    return x * (1 + scale) + shift, gate


def post_norm(x, y, gate, g, b):
    return layer_norm(DEEPNORM_ALPHA * x + (1 + gate) * y, g, b)


def split_cols(y, sizes):
    offs = [int(o) for o in np.cumsum(sizes)[:-1]]
    return jnp.split(y, offs, axis=-1)


def t5_bucket(dist):
    n = jnp.maximum(dist, 0)
    max_exact = REL_BUCKETS // 2
    nf = jnp.maximum(n, 1).astype(F32)
    log_b = max_exact + (jnp.log(nf / max_exact) / math.log(REL_MAX_DIST / max_exact) * (REL_BUCKETS - max_exact)).astype(jnp.int32)
    return jnp.where(n < max_exact, n, jnp.minimum(log_b, REL_BUCKETS - 1))


def rel_bias_shared(tab, dist):
    return jnp.moveaxis(tab[t5_bucket(dist)].astype(F32), -1, -3)


def group_heads(b):
    return b.reshape(NSA_KV_HEADS, NSA_GROUP, *b.shape[1:])


def masked_softmax(s, mask):
    s = jnp.where(mask, s, -jnp.inf)
    m = jnp.max(s, axis=-1, keepdims=True)
    p = jnp.exp(s - jnp.where(jnp.isfinite(m), m, 0.0))
    return p / jnp.maximum(jnp.sum(p, axis=-1, keepdims=True), jnp.finfo(F32).tiny)


def gather_past(pool, layer, pages):
    g = pool[layer, pages]
    return g.reshape(g.shape[0], g.shape[1] * g.shape[2], *g.shape[3:])


def diff_lambda_value(lam, lam_init):
    lam = lam.astype(F32)
    return jnp.exp(jnp.sum(lam[0] * lam[1])) - jnp.exp(jnp.sum(lam[2] * lam[3])) + lam_init


def diff_attend(q, segs, lam_full, subln_g, lam_init, tab):
    s = jnp.concatenate([jnp.einsum('bqhcd,bnhcd->bchqn', q, k).astype(F32) * DIFF_DK ** -0.5 + rel_bias_shared(tab, dist) for k, _, dist in segs], axis=-1)
    mask = jnp.concatenate([dist >= 0 for _, _, dist in segs], axis=-1)
    p = masked_softmax(s, mask)
    a = p[:, 0] - lam_full * p[:, 1]
    o = 0.0
    off = 0
    for _, v, dist in segs:
        n = dist.shape[1]
        o = o + jnp.einsum('bhqn,bnhd->bqhd', a[..., off:off + n].astype(v.dtype), v).astype(F32)
        off += n
    o = o * lax.rsqrt(jnp.mean(o * o, -1, keepdims=True) + RMS_EPS) * subln_g.astype(F32) * (1.0 - lam_init)
    return o.astype(q.dtype)


def diff_prompt(q, k, v, lam_full, subln_g, lam_init, tab):
    B, S = q.shape[:2]
    nqb = S // Q_BLOCK
    kpos = jnp.arange(S)
    qb = jnp.moveaxis(q.reshape(B, nqb, Q_BLOCK, *q.shape[2:]), 1, 0)

    def block(args):
        q_blk, i = args
        qpos = i * Q_BLOCK + jnp.arange(Q_BLOCK)
        return diff_attend(q_blk, ((k, v, qpos[:, None] - kpos[None, :]),), lam_full, subln_g, lam_init, tab)

    o = lax.map(block, (qb, jnp.arange(nqb)))
    return jnp.moveaxis(o, 0, 1).reshape(B, S, *o.shape[3:])


def attn_shared(q, k, v, bias, mask):
    s = jnp.einsum('...tgrd,...ngd->...grtn', q, k).astype(F32) * HEAD_DIM ** -0.5 + bias
    p = masked_softmax(s, mask)
    return jnp.einsum('...grtn,...ngd->...tgrd', p.astype(v.dtype), v), p


def nsa_compress(k_seq, pe, w1, w2):
    B, L, G, d = k_seq.shape
    n_cmp = (L - NSA_CMP_LEN) // NSA_CMP_STRIDE + 1
    idx = jnp.arange(n_cmp)[:, None] * NSA_CMP_STRIDE + jnp.arange(NSA_CMP_LEN)[None, :]
    blocks = k_seq[:, idx] + pe[:, None, :]
    flat = jnp.moveaxis(blocks, 3, 2).reshape(B, n_cmp, G, NSA_CMP_LEN * d)
    return jax.nn.gelu(flat @ w1) @ w2


def nsa_cmp_branch(q, qpos, kc_seq, vc_seq, cmp_pe, cmp_w1, cmp_w2, tab):
    L = kc_seq.shape[1]
    k_cmp = nsa_compress(kc_seq, cmp_pe[0], cmp_w1[0], cmp_w2[0])
    v_cmp = nsa_compress(vc_seq, cmp_pe[1], cmp_w1[1], cmp_w2[1])
    cmp_start = jnp.arange(k_cmp.shape[1]) * NSA_CMP_STRIDE
    dist = qpos[:, None] - (cmp_start + NSA_CMP_LEN - 1)[None, :]
    o_cmp, p_cmp = attn_shared(q, k_cmp, v_cmp, group_heads(rel_bias_shared(tab, dist)), dist >= 0)
    n_slc = -(-L // NSA_SEL_BLOCK)
    slc_start = jnp.arange(n_slc) * NSA_SEL_BLOCK
    covers = ((cmp_start[:, None] < slc_start[None, :] + NSA_SEL_BLOCK) & (cmp_start[:, None] + NSA_CMP_LEN > slc_start[None, :])).astype(F32)
    imp = jnp.einsum('bgrtn,nj->bgtj', p_cmp, covers)
    blk = jnp.arange(n_slc)[None, :]
    cur = (qpos // NSA_SEL_BLOCK)[:, None]
    valid = slc_start[None, :] <= qpos[:, None]
    forced = valid & ((blk == 0) | (blk == cur) | (blk == cur - 1))
    score = jnp.where(forced, jnp.inf, jnp.where(valid, imp, -jnp.inf))
    top_s, top_i = lax.top_k(score, min(NSA_SEL_TOPK, n_slc))
    return o_cmp, top_i, top_s > -jnp.inf


def nsa_sel_chunk(q, qpos, idx, valid, ks_blk, vs_blk, tab):
    B, G, T, K = idx.shape
    b_i = jnp.arange(B)[:, None, None, None]
    g_i = jnp.arange(G)[None, :, None, None]
    k_g = ks_blk[b_i, idx, :, g_i].reshape(B, G, T, K * NSA_SEL_BLOCK, HEAD_DIM)
    v_g = vs_blk[b_i, idx, :, g_i].reshape(B, G, T, K * NSA_SEL_BLOCK, HEAD_DIM)
    kpos = (idx[..., None] * NSA_SEL_BLOCK + jnp.arange(NSA_SEL_BLOCK)).reshape(B, G, T, -1)
    dist = qpos[None, None, :, None] - kpos
    mask = jnp.repeat(valid, NSA_SEL_BLOCK, axis=-1) & (dist >= 0)
    bias = tab.reshape(REL_BUCKETS, G, -1)[t5_bucket(dist), g_i].astype(F32)
    s = jnp.einsum('btgrd,bgtnd->bgrtn', q, k_g).astype(F32) * HEAD_DIM ** -0.5 + jnp.moveaxis(bias, -1, 2)
    p = masked_softmax(s, mask[:, :, None])
    return jnp.einsum('bgrtn,bgtnd->btgrd', p.astype(v_g.dtype), v_g)


def nsa_sel_branch(q, qpos, idx, valid, ks_seq, vs_seq, tab):
    B, T, G, R, d = q.shape
    L = ks_seq.shape[1]
    n_slc = -(-L // NSA_SEL_BLOCK)
    pad = ((0, 0), (0, n_slc * NSA_SEL_BLOCK - L), (0, 0), (0, 0))
    ks_blk = jnp.pad(ks_seq, pad).reshape(B, n_slc, NSA_SEL_BLOCK, G, d)
    vs_blk = jnp.pad(vs_seq, pad).reshape(B, n_slc, NSA_SEL_BLOCK, G, d)
    if T <= Q_BLOCK:
        return nsa_sel_chunk(q, qpos, idx, valid, ks_blk, vs_blk, tab)
    nc = T // Q_BLOCK
    K = idx.shape[-1]
    xs = (jnp.moveaxis(q.reshape(B, nc, Q_BLOCK, G, R, d), 1, 0), qpos.reshape(nc, Q_BLOCK),
          jnp.moveaxis(idx.reshape(B, G, nc, Q_BLOCK, K), 2, 0), jnp.moveaxis(valid.reshape(B, G, nc, Q_BLOCK, K), 2, 0))
    o = lax.map(lambda a: nsa_sel_chunk(a[0], a[1], a[2], a[3], ks_blk, vs_blk, tab), xs)
    return jnp.moveaxis(o, 0, 1).reshape(B, T, G, R, d)


def nsa_win_prompt(q, kw, vw, tab):
    B, T, G, R, d = q.shape
    nb = T // Q_BLOCK
    n_k = NSA_WINDOW + Q_BLOCK
    kidx = jnp.arange(nb)[:, None] * Q_BLOCK + jnp.arange(n_k)[None, :] - NSA_WINDOW
    kb = kw[:, jnp.clip(kidx, 0)]
    vb = vw[:, jnp.clip(kidx, 0)]
    dist = jnp.arange(Q_BLOCK)[:, None] + NSA_WINDOW - jnp.arange(n_k)[None, :]
    mask = (dist >= 0) & (dist < NSA_WINDOW) & (kidx >= 0)[:, None, None, None, :]
    o, _ = attn_shared(q.reshape(B, nb, Q_BLOCK, G, R, d), kb, vb, group_heads(rel_bias_shared(tab, dist)), mask)
    return o.reshape(B, T, G, R, d)


def nsa_win_buffer(q, qpos, kw, vw, kpos, tab):
    dist = qpos[:, None] - kpos[None, :]
    o, _ = attn_shared(q, kw, vw, group_heads(rel_bias_shared(tab, dist)), (dist >= 0) & (dist < NSA_WINDOW))
    return o


def nsa_combine(o_cmp, o_slc, o_win, g):
    B, T = o_cmp.shape[:2]
    gate = jax.nn.sigmoid(g.astype(F32))
    o = gate[..., 0:1] * o_cmp + gate[..., 1:2] * o_slc + gate[..., 2:3] * o_win
    return o.astype(o_cmp.dtype).reshape(B, T, -1)


def project_ab(h, w_in):
    B, T, _ = h.shape
    dq, dk, dv, nq, kc, vc, ks, vs, kw, vw, g = split_cols(h @ w_in, AB_SIZES)
    kvs = (B, T, NSA_KV_HEADS, HEAD_DIM)
    return (dq.reshape(B, T, DIFF_HEADS, 2, DIFF_DK), dk.reshape(B, T, DIFF_HEADS, 2, DIFF_DK), dv.reshape(B, T, DIFF_HEADS, HEAD_DIM),
            nq.reshape(B, T, NSA_KV_HEADS, NSA_GROUP, HEAD_DIM), kc.reshape(kvs), vc.reshape(kvs), ks.reshape(kvs), vs.reshape(kvs),
            kw.reshape(kvs), vw.reshape(kvs), g.reshape(B, T, NSA_KV_HEADS, NSA_GROUP, 3))


def ab_output(o_diff, o_nsa, w_out):
    B, T = o_nsa.shape[:2]
    return jnp.concatenate([o_diff.reshape(B, T, -1), o_nsa], axis=-1) @ w_out


def even_mixer_prompt(h, w_in, w_out, lam, subln_g, cmp_pe, cmp_w1, cmp_w2, rel_bias, lam_init):
    B, T, _ = h.shape
    dq, dk, dv, nq, kc, vc, ks, vs, kw, vw, g = project_ab(h, w_in)
    tab_d = rel_bias[:, :DIFF_HEADS]
    tab_n = rel_bias[:, DIFF_HEADS:DIFF_HEADS + NSA_HEADS]
    o_diff = diff_prompt(dq, dk, dv, diff_lambda_value(lam, lam_init), subln_g, lam_init, tab_d)
    qpos = jnp.arange(T)
    o_cmp, idx, valid = nsa_cmp_branch(nq, qpos, kc, vc, cmp_pe, cmp_w1, cmp_w2, tab_n)
    o_slc = nsa_sel_branch(nq, qpos, idx, valid, ks, vs, tab_n)
    o_win = nsa_win_prompt(nq, kw, vw, tab_n)
    y = ab_output(o_diff, nsa_combine(o_cmp, o_slc, o_win, g), w_out)
    keep = min(NSA_WINDOW, T)
    return y, (dk.reshape(B, T, DIFF_HEADS, 2 * DIFF_DK), dv, kc, vc, ks, vs, kw[:, T - keep:], vw[:, T - keep:])


def even_mixer_sample(h, page_table, i, cache_diff_k, cache_diff_v, cache_cmp_k, cache_cmp_v, cache_slc_k, cache_slc_v,
                      state_win_k, state_win_v, w_in, w_out, lam, subln_g, cmp_pe, cmp_w1, cmp_w2, rel_bias, lam_init):
    B, T, _ = h.shape
    past = page_table.shape[1] * PAGE_SIZE
    dq, dk, dv, nq, kc, vc, ks, vs, kw, vw, g = project_ab(h, w_in)
    tab_d = rel_bias[:, :DIFF_HEADS]
    tab_n = rel_bias[:, DIFF_HEADS:DIFF_HEADS + NSA_HEADS]
    qpos = past + jnp.arange(T)
    kd_past = gather_past(cache_diff_k, i, page_table).reshape(B, past, DIFF_HEADS, 2, DIFF_DK)
    vd_past = gather_past(cache_diff_v, i, page_table)
    segs = ((kd_past, vd_past, qpos[:, None] - jnp.arange(past)[None, :]), (dk, dv, qpos[:, None] - qpos[None, :]))
    o_diff = diff_attend(dq, segs, diff_lambda_value(lam, lam_init), subln_g, lam_init, tab_d)
    kc_seq = jnp.concatenate([gather_past(cache_cmp_k, i, page_table), kc], axis=1)
    vc_seq = jnp.concatenate([gather_past(cache_cmp_v, i, page_table), vc], axis=1)
    ks_seq = jnp.concatenate([gather_past(cache_slc_k, i, page_table), ks], axis=1)
    vs_seq = jnp.concatenate([gather_past(cache_slc_v, i, page_table), vs], axis=1)
    o_cmp, idx, valid = nsa_cmp_branch(nq, qpos, kc_seq, vc_seq, cmp_pe, cmp_w1, cmp_w2, tab_n)
    o_slc = nsa_sel_branch(nq, qpos, idx, valid, ks_seq, vs_seq, tab_n)
    kw_seq = jnp.concatenate([state_win_k[i], kw], axis=1)
    vw_seq = jnp.concatenate([state_win_v[i], vw], axis=1)
    kpos = past - state_win_k.shape[2] + jnp.arange(kw_seq.shape[1])
    o_win = nsa_win_buffer(nq, qpos, kw_seq, vw_seq, kpos, tab_n)
    y = ab_output(o_diff, nsa_combine(o_cmp, o_slc, o_win, g), w_out)
    keep = min(NSA_WINDOW, past + T)
    n_w = kw_seq.shape[1]
    return y, (dk.reshape(B, T, DIFF_HEADS, 2 * DIFF_DK), dv, kc, vc, ks, vs, kw_seq[:, n_w - keep:], vw_seq[:, n_w - keep:])


def moba_select(q, qpos, k_mean, n_sel):
    s = jnp.einsum('bqhd,bjhd->bqhj', q, k_mean).astype(F32)
    cand = jnp.arange(k_mean.shape[1])[None, :] < (qpos // MOBA_BLOCK)[:, None]
    s = jnp.where(cand[None, :, None, :], s, -jnp.inf)
    top_s, idx = lax.top_k(s, n_sel)
    return idx, top_s > -jnp.inf


def moba_core(q, qpos, k_sel, v_sel, idx, valid, k_own, v_own, pos_own, tab):
    scale = HEAD_DIM ** -0.5
    dist_own = qpos[:, None] - pos_own[None, :]
    s_own = jnp.einsum('bqhd,bnhd->bqhn', q, k_own).astype(F32) * scale + jnp.swapaxes(tab[t5_bucket(dist_own)].astype(F32), -1, -2)[None]
    m_own = jnp.broadcast_to((dist_own >= 0)[None, :, None, :], s_own.shape)
    if k_sel is None:
        p = masked_softmax(s_own, m_own)
        return jnp.einsum('bqhn,bnhd->bqhd', p.astype(v_own.dtype), v_own)
    H = q.shape[2]
    pos_sel = (idx[..., None] * MOBA_BLOCK + jnp.arange(MOBA_BLOCK)).reshape(*idx.shape[:3], -1)
    h_i = jnp.arange(H)[None, None, :, None]
    s_sel = jnp.einsum('bqhd,bqhnd->bqhn', q, k_sel).astype(F32) * scale + tab[t5_bucket(qpos[None, :, None, None] - pos_sel), h_i].astype(F32)
    m_sel = jnp.repeat(valid, MOBA_BLOCK, axis=-1)
    p = masked_softmax(jnp.concatenate([s_sel, s_own], axis=-1), jnp.concatenate([m_sel, m_own], axis=-1))
    n = s_sel.shape[-1]
    return (jnp.einsum('bqhn,bqhnd->bqhd', p[..., :n].astype(v_sel.dtype), v_sel)
            + jnp.einsum('bqhn,bnhd->bqhd', p[..., n:].astype(v_own.dtype), v_own))


def moba_prompt(q, k, v, tab):
    B, S, H, d = q.shape
    nb = -(-S // MOBA_BLOCK)
    pad = ((0, 0), (0, nb * MOBA_BLOCK - S), (0, 0), (0, 0))
    k_blk = jnp.pad(k, pad).reshape(B, nb, MOBA_BLOCK, H, d)
    v_blk = jnp.pad(v, pad).reshape(B, nb, MOBA_BLOCK, H, d)
    k_mean = jnp.mean(k_blk.astype(F32), axis=2).astype(k.dtype)
    n_sel = min(MOBA_TOPK, nb - 1)
    b_i = jnp.arange(B)[:, None, None, None]
    h_i = jnp.arange(H)[None, None, :, None]

    def chunk(args):
        qc, c = args
        qpos = c * MOBA_Q_CHUNK + jnp.arange(MOBA_Q_CHUNK)
        own = (c * MOBA_Q_CHUNK) // MOBA_BLOCK
        k_own = lax.dynamic_index_in_dim(k_blk, own, axis=1, keepdims=False)
        v_own = lax.dynamic_index_in_dim(v_blk, own, axis=1, keepdims=False)
        pos_own = own * MOBA_BLOCK + jnp.arange(MOBA_BLOCK)
        if n_sel == 0:
            return moba_core(qc, qpos, None, None, None, None, k_own, v_own, pos_own, tab)
        idx, valid = moba_select(qc, qpos, k_mean, n_sel)
        k_sel = k_blk[b_i, idx, :, h_i].reshape(B, MOBA_Q_CHUNK, H, n_sel * MOBA_BLOCK, d)
        v_sel = v_blk[b_i, idx, :, h_i].reshape(B, MOBA_Q_CHUNK, H, n_sel * MOBA_BLOCK, d)
        return moba_core(qc, qpos, k_sel, v_sel, idx, valid, k_own, v_own, pos_own, tab)

    nc = S // MOBA_Q_CHUNK
    o = lax.map(chunk, (jnp.moveaxis(q.reshape(B, nc, MOBA_Q_CHUNK, H, d), 1, 0), jnp.arange(nc)))
    return jnp.moveaxis(o, 0, 1).reshape(B, S, H, d)


def moba_sample(q, k_new, v_new, cache_k, cache_v, layer, page_table, tab):
    B, T, H, d = q.shape
    past = page_table.shape[1] * PAGE_SIZE
    ppb = MOBA_BLOCK // PAGE_SIZE
    nb_full = past // MOBA_BLOCK
    own_start = nb_full * MOBA_BLOCK
    own_pages = page_table[:, own_start // PAGE_SIZE:]
    k_own = jnp.concatenate([gather_past(cache_k, layer, own_pages), k_new], axis=1)
    v_own = jnp.concatenate([gather_past(cache_v, layer, own_pages), v_new], axis=1)
    pos_own = own_start + jnp.arange(k_own.shape[1])
    qpos = past + jnp.arange(T)
    n_sel = min(MOBA_TOPK, nb_full)
    if n_sel == 0:
        return moba_core(q, qpos, None, None, None, None, k_own, v_own, pos_own, tab)
    k_past = gather_past(cache_k, layer, page_table[:, :nb_full * ppb])
    k_mean = jnp.mean(k_past.reshape(B, nb_full, MOBA_BLOCK, H, d).astype(F32), axis=2).astype(q.dtype)
    b_i = jnp.arange(B)[:, None, None, None, None]
    h_i = jnp.arange(H)[None, None, :, None, None]

    def one_query(args):
        qc, qp = args
        idx, valid = moba_select(qc, qp, k_mean, n_sel)
        phys = page_table[b_i, idx[..., None] * ppb + jnp.arange(ppb)]
        k_sel = cache_k[layer, phys, :, h_i].reshape(B, 1, H, n_sel * MOBA_BLOCK, d)
        v_sel = cache_v[layer, phys, :, h_i].reshape(B, 1, H, n_sel * MOBA_BLOCK, d)
        return moba_core(qc, qp, k_sel, v_sel, idx, valid, k_own, v_own, pos_own, tab)

    o = lax.map(one_query, (jnp.moveaxis(q, 1, 0)[:, :, None], qpos[:, None]))
    return jnp.moveaxis(o[:, :, 0], 0, 1)


def odd_mixer_prompt(h, w_in, w_out, rel_bias):
    B, T, _ = h.shape
    q, k, v = [a.reshape(B, T, MOBA_HEADS, HEAD_DIM) for a in jnp.split(h @ w_in, 3, axis=-1)]
    o = moba_prompt(q, k, v, rel_bias[:, :MOBA_HEADS])
    return o.reshape(B, T, -1) @ w_out, (k, v)


def odd_mixer_sample(h, page_table, j, cache_k, cache_v, w_in, w_out, rel_bias):
    B, T, _ = h.shape
    q, k, v = [a.reshape(B, T, MOBA_HEADS, HEAD_DIM) for a in jnp.split(h @ w_in, 3, axis=-1)]
    o = moba_sample(q, k, v, cache_k, cache_v, j, page_table, rel_bias[:, :MOBA_HEADS])
    return o.reshape(B, T, -1) @ w_out, (k, v)


def hier_moe(h, wg_r, bg_r, we_r, be_r, w_gate, w_up, w_down):
    B, T, _ = h.shape
    pg = jax.nn.softmax((h @ wg_r + bg_r).astype(F32), axis=-1)
    pg_top, g_top = lax.top_k(pg, 1)
    le = (h @ we_r + be_r).astype(F32).reshape(B, T, N_GROUPS, EXPERTS_PER_GROUP)
    le_g = jnp.take_along_axis(le, g_top[..., None], axis=2)[:, :, 0]
    pw, e_top = lax.top_k(jax.nn.softmax(le_g, axis=-1), EXPERT_TOPK)
    w = pw / jnp.sum(pw, axis=-1, keepdims=True) * pg_top
    eid = g_top * EXPERTS_PER_GROUP + e_top
    combine = jnp.sum(jax.nn.one_hot(eid, N_EXPERTS, dtype=F32) * w[..., None], axis=-2)
    hid = jax.nn.silu(jnp.einsum('btd,edf->btef', h, w_gate)) * jnp.einsum('btd,edf->btef', h, w_up)
    hid = hid * combine[..., None].astype(h.dtype)
    return jnp.einsum('btef,efd->btd', hid, w_down)


def stack_field(rows, j):
    return jnp.stack([r[j] for r in rows])


def setup_inputs(seed: int = 0) -> dict:
    key = jax.random.key(seed)
    ks = iter(jax.random.split(key, 48))

    def nrm(shape, scale):
        return scale * jax.random.normal(next(ks), shape, F32)

    n_pages = PAST_LEN // PAGE_SIZE
    n_used = DEC_BATCH * n_pages
    n_pool = n_used + n_used // 4
    w_buf = min(NSA_WINDOW, PAST_LEN)
    page_table = jax.random.permutation(next(ks), n_pool)[:n_used].reshape(DEC_BATCH, n_pages).astype(jnp.int32)
    dkv = (N_EVEN, n_pool, PAGE_SIZE, NSA_KV_HEADS, HEAD_DIM)
    mix = N_HEADS * HEAD_DIM
    return {
        'x_prompt': nrm((BATCH, SEQ, D_MODEL), 1.0),
        'x_sample': nrm((DEC_BATCH, DEC_SEQ, D_MODEL), 1.0),
        'c_prompt': nrm((BATCH, D_MODEL), 1.0),
        'c_sample': nrm((DEC_BATCH, D_MODEL), 1.0),
        'page_table': page_table,
        'cache_diff_k': nrm((N_EVEN, n_pool, PAGE_SIZE, DIFF_HEADS, 2 * DIFF_DK), 1.0),
        'cache_diff_v': nrm((N_EVEN, n_pool, PAGE_SIZE, DIFF_HEADS, HEAD_DIM), 1.0),
        'cache_nsa_cmp_k': nrm(dkv, 1.0),
        'cache_nsa_cmp_v': nrm(dkv, 1.0),
        'cache_nsa_slc_k': nrm(dkv, 1.0),
        'cache_nsa_slc_v': nrm(dkv, 1.0),
        'state_nsa_win_k': nrm((N_EVEN, DEC_BATCH, w_buf, NSA_KV_HEADS, HEAD_DIM), 1.0),
        'state_nsa_win_v': nrm((N_EVEN, DEC_BATCH, w_buf, NSA_KV_HEADS, HEAD_DIM), 1.0),
        'cache_moba_k': nrm((N_ODD, n_pool, PAGE_SIZE, MOBA_HEADS, HEAD_DIM), 1.0),
        'cache_moba_v': nrm((N_ODD, n_pool, PAGE_SIZE, MOBA_HEADS, HEAD_DIM), 1.0),
        'w_in_ab': nrm((N_EVEN, D_MODEL, AB_COLS), D_MODEL ** -0.5),
        'w_out_ab': nrm((N_EVEN, mix, D_MODEL), DEEPNORM_BETA * mix ** -0.5),
        'diff_lambda': nrm((N_EVEN, 4, DIFF_DK), 0.1),
        'diff_subln_g': 1.0 + nrm((N_EVEN, HEAD_DIM), 0.02),
        'nsa_cmp_pe': nrm((N_EVEN, 2, NSA_CMP_LEN, HEAD_DIM), 0.1),
        'nsa_cmp_w1': nrm((N_EVEN, 2, NSA_CMP_LEN * HEAD_DIM, NSA_CMP_HIDDEN), (NSA_CMP_LEN * HEAD_DIM) ** -0.5),
        'nsa_cmp_w2': nrm((N_EVEN, 2, NSA_CMP_HIDDEN, HEAD_DIM), NSA_CMP_HIDDEN ** -0.5),
        'w_in_c': nrm((N_ODD, D_MODEL, 3 * MOBA_HEADS * HEAD_DIM), D_MODEL ** -0.5),
        'w_out_c': nrm((N_ODD, MOBA_HEADS * HEAD_DIM, D_MODEL), DEEPNORM_BETA * (MOBA_HEADS * HEAD_DIM) ** -0.5),
        'rel_bias': nrm((REL_BUCKETS, N_HEADS), 0.5),
        'ada_w': nrm((DEPTH, 2, D_MODEL, 3 * D_MODEL), 0.5 * D_MODEL ** -0.5),
        'ada_b': nrm((DEPTH, 2, 3 * D_MODEL), 0.02),
        'ln_g': 1.0 + nrm((DEPTH, 2, D_MODEL), 0.02),
        'ln_b': nrm((DEPTH, 2, D_MODEL), 0.02),
        'router_group_w': nrm((DEPTH, D_MODEL, N_GROUPS), D_MODEL ** -0.5),
        'router_group_b': nrm((DEPTH, N_GROUPS), 0.01),
        'router_expert_w': nrm((DEPTH, D_MODEL, N_EXPERTS), D_MODEL ** -0.5),
        'router_expert_b': nrm((DEPTH, N_EXPERTS), 0.01),
        'moe_w_gate': nrm((DEPTH, N_EXPERTS, D_MODEL, D_EXPERT), D_MODEL ** -0.5),
        'moe_w_up': nrm((DEPTH, N_EXPERTS, D_MODEL, D_EXPERT), D_MODEL ** -0.5),
        'moe_w_down': nrm((DEPTH, N_EXPERTS, D_EXPERT, D_MODEL), DEEPNORM_BETA * D_EXPERT ** -0.5),
    }


def reference(x_prompt, x_sample, c_prompt, c_sample, page_table, cache_diff_k, cache_diff_v, cache_nsa_cmp_k, cache_nsa_cmp_v,
              cache_nsa_slc_k, cache_nsa_slc_v, state_nsa_win_k, state_nsa_win_v, cache_moba_k, cache_moba_v,
              w_in_ab, w_out_ab, diff_lambda, diff_subln_g, nsa_cmp_pe, nsa_cmp_w1, nsa_cmp_w2, w_in_c, w_out_c, rel_bias,
              ada_w, ada_b, ln_g, ln_b, router_group_w, router_group_b, router_expert_w, router_expert_b,
              moe_w_gate, moe_w_up, moe_w_down):
    xp, xs = x_prompt, x_sample
    even_p, even_s, odd_p, odd_s = [], [], [], []
    for l in range(DEPTH):
        i = l // 2
        hp, gp = ada_in(xp, c_prompt, ada_w[l, 0], ada_b[l, 0])
        hs, gs = ada_in(xs, c_sample, ada_w[l, 0], ada_b[l, 0])
        if l % 2 == 0:
            lam_init = 0.8 - 0.6 * math.exp(-0.3 * l)
            wts = (w_in_ab[i], w_out_ab[i], diff_lambda[i], diff_subln_g[i], nsa_cmp_pe[i], nsa_cmp_w1[i], nsa_cmp_w2[i], rel_bias, lam_init)
            yp, sp = even_mixer_prompt(hp, *wts)
            ys, ss = even_mixer_sample(hs, page_table, i, cache_diff_k, cache_diff_v, cache_nsa_cmp_k, cache_nsa_cmp_v,
                                       cache_nsa_slc_k, cache_nsa_slc_v, state_nsa_win_k, state_nsa_win_v, *wts)
            even_p.append(sp)
            even_s.append(ss)
        else:
            yp, sp = odd_mixer_prompt(hp, w_in_c[i], w_out_c[i], rel_bias)
            ys, ss = odd_mixer_sample(hs, page_table, i, cache_moba_k, cache_moba_v, w_in_c[i], w_out_c[i], rel_bias)
            odd_p.append(sp)
            odd_s.append(ss)
        xp = post_norm(xp, yp, gp, ln_g[l, 0], ln_b[l, 0])
        xs = post_norm(xs, ys, gs, ln_g[l, 0], ln_b[l, 0])
        moe_w = (router_group_w[l], router_group_b[l], router_expert_w[l], router_expert_b[l], moe_w_gate[l], moe_w_up[l], moe_w_down[l])
        hp, gp = ada_in(xp, c_prompt, ada_w[l, 1], ada_b[l, 1])
        hs, gs = ada_in(xs, c_sample, ada_w[l, 1], ada_b[l, 1])
        xp = post_norm(xp, hier_moe(hp, *moe_w), gp, ln_g[l, 1], ln_b[l, 1])
        xs = post_norm(xs, hier_moe(hs, *moe_w), gs, ln_g[l, 1], ln_b[l, 1])
    diff_k_p, diff_k_s = stack_field(even_p, 0), stack_field(even_s, 0)
    diff_v_p, diff_v_s = stack_field(even_p, 1), stack_field(even_s, 1)
    cmp_k_p, cmp_k_s = stack_field(even_p, 2), stack_field(even_s, 2)
    cmp_v_p, cmp_v_s = stack_field(even_p, 3), stack_field(even_s, 3)
    slc_k_p, slc_k_s = stack_field(even_p, 4), stack_field(even_s, 4)
    slc_v_p, slc_v_s = stack_field(even_p, 5), stack_field(even_s, 5)
    win_k_p, win_k_s = stack_field(even_p, 6), stack_field(even_s, 6)
    win_v_p, win_v_s = stack_field(even_p, 7), stack_field(even_s, 7)
    moba_k_p, moba_k_s = stack_field(odd_p, 0), stack_field(odd_s, 0)
    moba_v_p, moba_v_s = stack_field(odd_p, 1), stack_field(odd_s, 1)
    return (xp, xs, diff_k_p, diff_k_s, diff_v_p, diff_v_s, cmp_k_p, cmp_k_s, cmp_v_p, cmp_v_s,
            slc_k_p, slc_k_s, slc_v_p, slc_v_s, win_k_p, win_k_s, win_v_p, win_v_s,
            moba_k_p, moba_k_s, moba_v_p, moba_v_s)
```

```python
import functools
import math

import numpy as np
import jax
import jax.numpy as jnp
from jax import lax
from jax.experimental import pallas as pl
from jax.experimental.pallas import tpu as pltpu

F32 = jnp.float32
BF16 = jnp.bfloat16
NEG = -1e30

HEAD_DIM = 128
DIFF_HEADS = 8
DIFF_DK = 64
NSA_HEADS = 8
NSA_KV_HEADS = 2
NSA_GROUP = 4
NSA_CMP_LEN = 32
NSA_CMP_STRIDE = 16
NSA_CMP_HIDDEN = 256
NSA_SEL_BLOCK = 64
NSA_SEL_TOPK = 16
NSA_WINDOW = 512
MOBA_HEADS = 16
MOBA_BLOCK = 256
MOBA_TOPK = 3
PAGE_SIZE = 128
REL_BUCKETS = 32
REL_MAX_DIST = 128
N_GROUPS = 4
EXPERTS_PER_GROUP = 4
N_EXPERTS = 16
DEPTH = 2
DEEPNORM_ALPHA = (2 * DEPTH) ** 0.25
LN_EPS = 1e-5
RMS_EPS = 1e-6

LANES = 128
ATT_TILE = 256
VMEM_LIMIT = 56 * 1024 * 1024


def _params(n_axes, vmem=VMEM_LIMIT):
    return pltpu.CompilerParams(dimension_semantics=("arbitrary",) * n_axes, vmem_limit_bytes=vmem)


def _dot(a, b):
    return jnp.dot(a, b, preferred_element_type=F32)


def _dot_nt(a, b):
    return lax.dot_general(a, b, (((1,), (1,)), ((), ())), preferred_element_type=F32)


def _ada_kernel(c_ref, w_ref, b_ref, o_ref):
    c = c_ref[...]
    a = (c * jax.nn.sigmoid(c)).astype(BF16)
    o_ref[...] = _dot(a, w_ref[...].astype(BF16)) + b_ref[...]


def ada_mod(c_all, ada_w, ada_b):
    n_sub, d, d3 = ada_w.shape
    bc = c_all.shape[0]
    tn = 768
    return pl.pallas_call(
        _ada_kernel,
        grid=(n_sub, d3 // tn),
        in_specs=[pl.BlockSpec((bc, d), lambda s, j: (0, 0)),
                  pl.BlockSpec((None, d, tn), lambda s, j: (s, 0, j)),
                  pl.BlockSpec((None, 1, tn), lambda s, j: (s, 0, j))],
        out_specs=pl.BlockSpec((None, bc, tn), lambda s, j: (s, 0, j)),
        out_shape=jax.ShapeDtypeStruct((n_sub, bc, d3), F32),
        compiler_params=_params(2),
        name="ada_mod",
    )(c_all, ada_w, ada_b.reshape(n_sub, 1, d3))


def _seg_spec(vec, m, tm):
    g, tg, d = vec.shape
    if tg == 1:
        per = m // g // tm
        return pl.BlockSpec((None, 1, d), lambda i, *_: (i // per, 0, 0))
    return pl.BlockSpec((None, tm, d), lambda i, *_: (0, i, 0))


def _mod_kernel(x_ref, sh_ref, sc_ref, o_ref):
    o_ref[...] = (x_ref[...] * (1 + sc_ref[...]) + sh_ref[...]).astype(BF16)


def mod_cast(x, shift, scale, tm):
    m, d = x.shape
    return pl.pallas_call(
        _mod_kernel,
        grid=(m // tm,),
        in_specs=[pl.BlockSpec((tm, d), lambda i: (i, 0)), _seg_spec(shift, m, tm), _seg_spec(scale, m, tm)],
        out_specs=pl.BlockSpec((tm, d), lambda i: (i, 0)),
        out_shape=jax.ShapeDtypeStruct((m, d), BF16),
        compiler_params=_params(1),
        name="mod_cast",
    )(x, shift, scale)


def _mm_kernel(a_ref, w_ref, o_ref):
    o_ref[...] = _dot(a_ref[...], w_ref[...]).astype(o_ref.dtype)


def matmul(a, w, n_cols, tm, tn, name):
    m, k = a.shape
    return pl.pallas_call(
        _mm_kernel,
        grid=(n_cols // tn, m // tm),
        in_specs=[pl.BlockSpec((tm, k), lambda j, i: (i, 0)), pl.BlockSpec((k, tn), lambda j, i: (0, j))],
        out_specs=pl.BlockSpec((tm, tn), lambda j, i: (i, j)),
        out_shape=jax.ShapeDtypeStruct((m, n_cols), F32),
        compiler_params=_params(2),
        name=name,
    )(a, w)


def _postnorm(x, y, gate, g, b):
    z = DEEPNORM_ALPHA * x + (1 + gate) * y
    zc = z - jnp.mean(z, -1, keepdims=True)
    var = jnp.mean(zc * zc, -1, keepdims=True)
    return zc * lax.rsqrt(var + LN_EPS) * g + b


def _outln_kernel(oa_ref, ob_ref, w_ref, x_ref, gate_ref, g_ref, b_ref, out_ref):
    half = oa_ref.shape[1]
    y = _dot(oa_ref[...], w_ref[:half, :]) + _dot(ob_ref[...], w_ref[half:, :])
    out_ref[...] = _postnorm(x_ref[...], y, gate_ref[...], g_ref[...], b_ref[...])


def out_proj_norm(oa, ob, cols_a, cols_b, w_out, x, gate, ln_g, ln_b, tm):
    m, d = x.shape
    half = w_out.shape[0] // 2
    return pl.pallas_call(
        _outln_kernel,
        grid=(m // tm,),
        in_specs=[pl.BlockSpec((tm, half), lambda i: (i, cols_a)),
                  pl.BlockSpec((tm, half), lambda i: (i, cols_b)),
                  pl.BlockSpec(w_out.shape, lambda i: (0, 0)),
                  pl.BlockSpec((tm, d), lambda i: (i, 0)),
                  _seg_spec(gate, m, tm),
                  pl.BlockSpec((1, d), lambda i: (0, 0)),
                  pl.BlockSpec((1, d), lambda i: (0, 0))],
        out_specs=pl.BlockSpec((tm, d), lambda i: (i, 0)),
        out_shape=jax.ShapeDtypeStruct((m, d), F32),
        compiler_params=_params(1),
        name="out_proj_norm",
    )(oa, ob, w_out, x, gate, ln_g.reshape(1, d), ln_b.reshape(1, d))


def _first_argmax(v, lane, valid):
    vm = jnp.where(valid, v, -jnp.inf)
    top = jnp.max(vm, -1, keepdims=True)
    idx = jnp.min(jnp.where(valid & (vm == top), lane, 1 << 20), -1, keepdims=True)
    return top, idx


def _moe_kernel(h_ref, rw_ref, rb_ref, wg_ref, wu_ref, wd_ref, x_ref, gate_ref, g_ref, b_ref, out_ref,
                comb_ref, acc_ref):
    e = pl.program_id(1)
    h = h_ref[...]

    @pl.when(e == 0)
    def _route():
        logits = _dot(h, rw_ref[...]) + rb_ref[...]
        lane = lax.broadcasted_iota(jnp.int32, logits.shape, 1)
        is_g = lane < N_GROUPS
        gmax = jnp.max(jnp.where(is_g, logits, -jnp.inf), -1, keepdims=True)
        gex = jnp.where(is_g, jnp.exp(logits - gmax), 0.0)
        pg = gex / jnp.sum(gex, -1, keepdims=True)
        pg_top, g_top = _first_argmax(pg, lane, is_g)
        ex_id = lane - N_GROUPS
        in_grp = (ex_id >= g_top * EXPERTS_PER_GROUP) & (ex_id < (g_top + 1) * EXPERTS_PER_GROUP)
        emax = jnp.max(jnp.where(in_grp, logits, -jnp.inf), -1, keepdims=True)
        eex = jnp.where(in_grp, jnp.exp(logits - emax), 0.0)
        pe = eex / jnp.sum(eex, -1, keepdims=True)
        p1, i1 = _first_argmax(pe, lane, in_grp)
        p2, i2 = _first_argmax(pe, lane, in_grp & (lane != i1))
        tot = p1 + p2
        comb = jnp.where(lane == i1, p1 / tot * pg_top, 0.0) + jnp.where(lane == i2, p2 / tot * pg_top, 0.0)
        comb_ref[...] = comb
        acc_ref[...] = jnp.zeros_like(acc_ref)

    lane = lax.broadcasted_iota(jnp.int32, comb_ref.shape, 1)
    c_e = jnp.sum(jnp.where(lane == e + N_GROUPS, comb_ref[...], 0.0), -1, keepdims=True)
    a = _dot(h, wg_ref[...])
    u = _dot(h, wu_ref[...])
    hid = (a * jax.nn.sigmoid(a)) * u * c_e
    acc_ref[...] += _dot(hid.astype(BF16), wd_ref[...])

    @pl.when(e == pl.num_programs(1) - 1)
    def _finish():
        out_ref[...] = _postnorm(x_ref[...], acc_ref[...], gate_ref[...], g_ref[...], b_ref[...])


def _router_pack(wg_r, bg_r, we_r, be_r):
    d = wg_r.shape[0]
    pad = LANES - N_GROUPS - N_EXPERTS
    rw = jnp.concatenate([wg_r, we_r, jnp.zeros((d, pad), F32)], axis=1).astype(BF16)
    rb = jnp.concatenate([bg_r, be_r, jnp.zeros((pad,), F32)]).reshape(1, LANES)
    return rw, rb


def moe_norm(h, rw, rb, wg, wu, wd, x, gate, ln_g, ln_b, tm):
    m, d = x.shape
    n_e, _, f = wg.shape
    return pl.pallas_call(
        _moe_kernel,
        grid=(m // tm, n_e),
        in_specs=[pl.BlockSpec((tm, d), lambda i, e: (i, 0)),
                  pl.BlockSpec(rw.shape, lambda i, e: (0, 0)),
                  pl.BlockSpec(rb.shape, lambda i, e: (0, 0)),
                  pl.BlockSpec((None, d, f), lambda i, e: (e, 0, 0)),
                  pl.BlockSpec((None, d, f), lambda i, e: (e, 0, 0)),
                  pl.BlockSpec((None, f, d), lambda i, e: (e, 0, 0)),
                  pl.BlockSpec((tm, d), lambda i, e: (i, 0)),
                  _seg_spec(gate, m, tm),
                  pl.BlockSpec((1, d), lambda i, e: (0, 0)),
                  pl.BlockSpec((1, d), lambda i, e: (0, 0))],
        out_specs=pl.BlockSpec((tm, d), lambda i, e: (i, 0)),
        out_shape=jax.ShapeDtypeStruct((m, d), F32),
        scratch_shapes=[pltpu.VMEM((tm, LANES), F32), pltpu.VMEM((tm, d), F32)],
        compiler_params=_params(2),
        name="moe_norm",
    )(h, rw, rb, wg, wu, wd, x, gate, ln_g.reshape(1, d), ln_b.reshape(1, d))


def _t5_bucket(dist):
    n = jnp.maximum(dist, 0)
    max_exact = REL_BUCKETS // 2
    nf = jnp.maximum(n, 1).astype(F32)
    log_b = max_exact + (jnp.log(nf / max_exact) / math.log(REL_MAX_DIST / max_exact) * (REL_BUCKETS - max_exact)).astype(jnp.int32)
    return jnp.where(n < max_exact, n, jnp.minimum(log_b, REL_BUCKETS - 1))


def _bias_of(tab, dist, mask):
    b = jnp.moveaxis(tab[_t5_bucket(dist)].astype(F32), -1, 0)
    return jnp.where(mask, b, NEG)


def _toeplitz_tiles(tab, t):
    i = jnp.arange(t)[:, None]
    j = jnp.arange(t)[None, :]
    diag = _bias_of(tab, i - j, i >= j)
    left = _bias_of(tab, t + i - j, jnp.ones((t, t), bool))
    return jnp.stack([diag, left], axis=1)


def _far_bias(tab):
    return jnp.broadcast_to(tab[REL_BUCKETS - 1].astype(F32)[:, None, None], (tab.shape[1], 1, LANES))


def _flash_init(m_ref, l_ref, acc_ref):
    m_ref[...] = jnp.full_like(m_ref, NEG)
    l_ref[...] = jnp.zeros_like(l_ref)
    acc_ref[...] = jnp.zeros_like(acc_ref)


def _flash_step(s, vb, m_ref, l_ref, acc_ref):
    m_old = m_ref[...]
    m_new = jnp.maximum(m_old, jnp.max(s, -1, keepdims=True))
    alpha = jnp.exp(m_old - m_new)
    p = jnp.exp(s - m_new)
    l_ref[...] = alpha * l_ref[...] + jnp.sum(p, -1, keepdims=True)
    acc_ref[...] = alpha * acc_ref[...] + _dot(p.astype(BF16), vb)
    m_ref[...] = m_new


def _diff_prompt_kernel(q_ref, k_ref, v_ref, bias_ref, far_ref, lam_ref, sub_ref, o_ref,
                        m1, l1, a1, m2, l2, a2, *, lam_init):
    t = q_ref.shape[0]
    qt = pl.program_id(2)
    q = q_ref[...]
    lane = lax.broadcasted_iota(jnp.int32, q.shape, 1)
    q1 = jnp.where(lane < DIFF_DK, q, 0.0).astype(BF16)
    q2 = jnp.where(lane >= DIFF_DK, q, 0.0).astype(BF16)
    scale = DIFF_DK ** -0.5
    _flash_init(m1, l1, a1)
    _flash_init(m2, l2, a2)

    def tile(kt, bias):
        kb = k_ref[pl.ds(pl.multiple_of(kt * t, t), t), :].astype(BF16)
        vb = v_ref[pl.ds(pl.multiple_of(kt * t, t), t), :].astype(BF16)
        _flash_step(_dot_nt(q1, kb) * scale + bias, vb, m1, l1, a1)
        _flash_step(_dot_nt(q2, kb) * scale + bias, vb, m2, l2, a2)

    tile(qt, bias_ref[0])

    @pl.when(qt >= 1)
    def _left():
        tile(qt - 1, bias_ref[1])

    far = far_ref[...][:, :1]

    def far_body(kt, carry):
        tile(kt, far)
        return carry

    lax.fori_loop(0, jnp.maximum(qt - 1, 0), far_body, 0)

    lam = lam_ref[...]
    lam_full = (jnp.exp(jnp.sum(lam[0:1] * lam[1:2], keepdims=True))
                - jnp.exp(jnp.sum(lam[2:3] * lam[3:4], keepdims=True)) + lam_init)
    o = a1[...] / l1[...] - lam_full * (a2[...] / l2[...])
    o = o * lax.rsqrt(jnp.mean(o * o, -1, keepdims=True) + RMS_EPS) * sub_ref[...] * (1.0 - lam_init)
    o_ref[...] = o.astype(o_ref.dtype)


def diff_prompt(proj, tiles, far, lam, subln, lam_init):
    b, s, _ = proj.shape
    t = ATT_TILE
    kern = functools.partial(_diff_prompt_kernel, lam_init=lam_init)
    return pl.pallas_call(
        kern,
        grid=(b, DIFF_HEADS, s // t),
        in_specs=[pl.BlockSpec((None, t, HEAD_DIM), lambda bi, h, qt: (bi, qt, h)),
                  pl.BlockSpec((None, s, HEAD_DIM), lambda bi, h, qt: (bi, 0, DIFF_HEADS + h)),
                  pl.BlockSpec((None, s, HEAD_DIM), lambda bi, h, qt: (bi, 0, 2 * DIFF_HEADS + h)),
                  pl.BlockSpec((None, 2, t, t), lambda bi, h, qt: (h, 0, 0, 0)),
                  pl.BlockSpec((None, 1, LANES), lambda bi, h, qt: (h, 0, 0)),
                  pl.BlockSpec(lam.shape, lambda bi, h, qt: (0, 0)),
                  pl.BlockSpec((1, HEAD_DIM), lambda bi, h, qt: (0, 0))],
        out_specs=pl.BlockSpec((None, t, HEAD_DIM), lambda bi, h, qt: (bi, qt, h)),
        out_shape=jax.ShapeDtypeStruct((b, s, DIFF_HEADS * HEAD_DIM), BF16),
        scratch_shapes=[pltpu.VMEM((t, 1), F32), pltpu.VMEM((t, 1), F32), pltpu.VMEM((t, HEAD_DIM), F32)] * 2,
        compiler_params=_params(3),
        name="diff_prompt",
    )(proj, proj, proj, tiles, far, lam, subln.reshape(1, HEAD_DIM))


def _cmp_ab_kernel(x_ref, pe_ref, w1_ref, o_ref):
    half = w1_ref.shape[0] // 2
    hid = w1_ref.shape[1]
    for g in range(NSA_KV_HEADS):
        xg = jnp.concatenate([x_ref[:, (NSA_KV_HEADS * p + g) * HEAD_DIM:(NSA_KV_HEADS * p + g + 1) * HEAD_DIM]
                              for p in range(NSA_CMP_STRIDE)], axis=1)
        top = (xg + pe_ref[0:1, :]).astype(BF16)
        bot = (xg + pe_ref[1:2, :]).astype(BF16)
        o_ref[:, 2 * g * hid:(2 * g + 1) * hid] = _dot(top, w1_ref[:half, :])
        o_ref[:, (2 * g + 1) * hid:(2 * g + 2) * hid] = _dot(bot, w1_ref[half:, :])


def _cmp_fin_kernel(*refs, n_prefetch):
    ab_refs, w2_ref, o_ref = refs[n_prefetch:-2], refs[-2], refs[-1]
    ab = jnp.concatenate([r[...] for r in ab_refs], axis=0) if len(ab_refs) > 1 else ab_refs[0][...]
    n = ab.shape[0]
    hid = w2_ref.shape[0]
    row = lax.broadcasted_iota(jnp.int32, (n, HEAD_DIM), 0)
    for g in range(NSA_KV_HEADS):
        a = ab[:, 2 * g * hid:(2 * g + 1) * hid]
        b = ab[:, (2 * g + 1) * hid:(2 * g + 2) * hid]
        hidden = jax.nn.gelu(a + pltpu.roll(b, n - 1, 0), approximate=True)
        out = _dot(hidden.astype(BF16), w2_ref[...])
        o_ref[g] = jnp.where(row < n - 1, out, 0.0)


def nsa_compress(kseq, pe, w1, w2, page_table):
    rows = kseq.shape[0] * kseq.shape[1] // NSA_CMP_STRIDE
    width = NSA_CMP_STRIDE * NSA_KV_HEADS * HEAD_DIM
    x = kseq.reshape(rows, width)
    hid = w1.shape[1]
    tr = math.gcd(rows, 256)
    ab = pl.pallas_call(
        _cmp_ab_kernel,
        grid=(rows // tr,),
        in_specs=[pl.BlockSpec((tr, width), lambda i: (i, 0)),
                  pl.BlockSpec((2, width // 2), lambda i: (0, 0)),
                  pl.BlockSpec(w1.shape, lambda i: (0, 0))],
        out_specs=pl.BlockSpec((tr, 4 * hid), lambda i: (i, 0)),
        out_shape=jax.ShapeDtypeStruct((rows, 4 * hid), F32),
        compiler_params=_params(1),
        name="nsa_cmp_layer1",
    )(x, pe.reshape(2, width // 2), w1)
    n_cmp = 128
    out_block = pl.BlockSpec((None, NSA_KV_HEADS, n_cmp, HEAD_DIM), lambda b, *_: (b, 0, 0, 0))
    if page_table is None:
        nb = kseq.shape[0]
        return pl.pallas_call(
            functools.partial(_cmp_fin_kernel, n_prefetch=0),
            grid=(nb,),
            in_specs=[pl.BlockSpec((n_cmp, 4 * hid), lambda b: (b, 0)), pl.BlockSpec(w2.shape, lambda b: (0, 0))],
            out_specs=out_block,
            out_shape=jax.ShapeDtypeStruct((nb, NSA_KV_HEADS, n_cmp, HEAD_DIM), F32),
            compiler_params=_params(1),
            name="nsa_cmp_layer2",
        )(ab, w2)
    nb, n_pages = page_table.shape
    per = PAGE_SIZE // NSA_CMP_STRIDE
    ab3 = ab.reshape(rows // per, per, 4 * hid)
    grid_spec = pltpu.PrefetchScalarGridSpec(
        num_scalar_prefetch=1,
        grid=(nb,),
        in_specs=[pl.BlockSpec((None, per, 4 * hid), lambda b, pt, p=p: (pt[b, p], 0, 0)) for p in range(n_pages)]
        + [pl.BlockSpec(w2.shape, lambda b, pt: (0, 0))],
        out_specs=out_block,
    )
    return pl.pallas_call(
        functools.partial(_cmp_fin_kernel, n_prefetch=1),
        grid_spec=grid_spec,
        out_shape=jax.ShapeDtypeStruct((nb, NSA_KV_HEADS, n_cmp, HEAD_DIM), F32),
        compiler_params=_params(1),
        name="nsa_cmp_layer2_paged",
    )(page_table, *([ab3] * n_pages), w2)


def _rank_select(score, lane, n_valid_lanes, topk):
    rank = jnp.zeros(score.shape, jnp.int32)
    for j in range(n_valid_lanes):
        col = score[:, j:j + 1]
        beats = (col > score) | ((col == score) & (j < lane))
        rank = rank + beats.astype(jnp.int32)
    return rank < topk


def _nsa_cmp_kernel(q_ref, kc_ref, vc_ref, bias_ref, cov_ref, o_ref, sel_ref, *, qpos0, n_slc):
    t = q_ref.shape[0]
    qt = pl.program_id(2)
    kc = kc_ref[...].astype(BF16)
    vc = vc_ref[...].astype(BF16)
    scale = HEAD_DIM ** -0.5
    imp = jnp.zeros((t, LANES), F32)
    for r in range(NSA_GROUP):
        qr = q_ref[:, r * HEAD_DIM:(r + 1) * HEAD_DIM].astype(BF16)
        bias = bias_ref[r]
        live = bias > 0.5 * NEG
        s = _dot_nt(qr, kc) * scale + bias
        m = jnp.max(s, -1, keepdims=True)
        p = jnp.where(live, jnp.exp(s - m), 0.0)
        p = p / jnp.maximum(jnp.sum(p, -1, keepdims=True), jnp.finfo(F32).tiny)
        pb = p.astype(BF16)
        o_ref[:, r * HEAD_DIM:(r + 1) * HEAD_DIM] = _dot(pb, vc)
        imp = imp + _dot(pb, cov_ref[...])
    lane = lax.broadcasted_iota(jnp.int32, (t, LANES), 1)
    qpos = qpos0 + qt * t + lax.broadcasted_iota(jnp.int32, (t, LANES), 0)
    cur = qpos // NSA_SEL_BLOCK
    valid = (lane * NSA_SEL_BLOCK <= qpos) & (lane < n_slc)
    forced = valid & ((lane == 0) | (lane == cur) | (lane == cur - 1))
    score = jnp.where(forced, jnp.inf, jnp.where(valid, imp, -jnp.inf))
    chosen = _rank_select(score, lane, n_slc, min(NSA_SEL_TOPK, n_slc))
    sel_ref[...] = (valid & chosen).astype(F32)


def nsa_cmp_select(q_arr, q_col0, t, kcmp, vcmp, tab_n, qpos0, seq_len):
    b, sq, _ = q_arr.shape
    n_cmp = (seq_len - NSA_CMP_LEN) // NSA_CMP_STRIDE + 1
    n_slc = -(-seq_len // NSA_SEL_BLOCK)
    i = jnp.arange(LANES)
    qpos = qpos0 + jnp.arange(sq)
    dist = qpos[:, None] - (i * NSA_CMP_STRIDE + NSA_CMP_LEN - 1)[None, :]
    bias = _bias_of(tab_n, dist, (dist >= 0) & (i < n_cmp)[None, :]).reshape(NSA_KV_HEADS, NSA_GROUP, sq, LANES)
    cs = np.arange(LANES)[:, None] * NSA_CMP_STRIDE
    ss = np.arange(LANES)[None, :] * NSA_SEL_BLOCK
    covers = ((cs < ss + NSA_SEL_BLOCK) & (cs + NSA_CMP_LEN > ss)
              & (np.arange(LANES)[:, None] < n_cmp) & (np.arange(LANES)[None, :] < n_slc))
    covers = jnp.asarray(covers, BF16)
    kern = functools.partial(_nsa_cmp_kernel, qpos0=qpos0, n_slc=n_slc)
    gw = NSA_GROUP * HEAD_DIM
    return pl.pallas_call(
        kern,
        grid=(b, NSA_KV_HEADS, sq // t),
        in_specs=[pl.BlockSpec((None, t, gw), lambda bi, g, qt: (bi, qt, q_col0 + g)),
                  pl.BlockSpec((None, None, LANES, HEAD_DIM), lambda bi, g, qt: (bi, g, 0, 0)),
                  pl.BlockSpec((None, None, LANES, HEAD_DIM), lambda bi, g, qt: (bi, g, 0, 0)),
                  pl.BlockSpec((None, NSA_GROUP, t, LANES), lambda bi, g, qt: (g, 0, qt, 0)),
                  pl.BlockSpec((LANES, LANES), lambda bi, g, qt: (0, 0))],
        out_specs=[pl.BlockSpec((None, t, gw), lambda bi, g, qt: (bi, qt, g)),
                   pl.BlockSpec((None, None, t, LANES), lambda bi, g, qt: (bi, g, qt, 0))],
        out_shape=[jax.ShapeDtypeStruct((b, sq, NSA_KV_HEADS * gw), F32),
                   jax.ShapeDtypeStruct((b, NSA_KV_HEADS, sq, LANES), F32)],
        compiler_params=_params(3),
        name="nsa_cmp_select",
    )(q_arr, kcmp, vcmp, bias, covers)


def _gate_pad(g):
    lead = g.shape[:-1]
    g = g.reshape(*lead, NSA_KV_HEADS, NSA_GROUP * 3)
    g = jnp.pad(g, [(0, 0)] * (len(lead) + 1) + [(0, LANES - NSA_GROUP * 3)])
    return g.reshape(*lead, NSA_KV_HEADS * LANES)


def _nsa_prompt_kernel(q_ref, ks_ref, vs_ref, kw_ref, vw_ref, sel_ref, ocmp_ref, g_ref, bias_ref, far_ref, e_ref,
                       o_ref, selk_ref, oslc_ref, m_ref, l_ref, acc_ref):
    t = q_ref.shape[0]
    qt = pl.program_id(2)
    scale = HEAD_DIM ** -0.5
    qb = [q_ref[:, r * HEAD_DIM:(r + 1) * HEAD_DIM].astype(BF16) for r in range(NSA_GROUP)]
    selb = sel_ref[...].astype(BF16)
    for kt in range(selk_ref.shape[0]):
        selk_ref[kt] = _dot(selb, e_ref[:, kt * t:(kt + 1) * t])

    def reset():
        for r in range(NSA_GROUP):
            _flash_init(m_ref.at[r], l_ref.at[r], acc_ref.at[r])

    def tile(k_ref, v_ref, kt, logits_fn):
        start = pl.multiple_of(kt * t, t)
        kb = k_ref[pl.ds(start, t), :].astype(BF16)
        vb = v_ref[pl.ds(start, t), :].astype(BF16)
        for r in range(NSA_GROUP):
            s = logits_fn(_dot_nt(qb[r], kb) * scale, r)
            _flash_step(s, vb, m_ref.at[r], l_ref.at[r], acc_ref.at[r])

    def far_of(r):
        return far_ref[r:r + 1, 0:1]

    reset()
    tile(ks_ref, vs_ref, qt, lambda s, r: jnp.where(selk_ref[qt] > 0.5, s + bias_ref[r, 0], NEG))

    @pl.when(qt >= 1)
    def _slc_left():
        tile(ks_ref, vs_ref, qt - 1, lambda s, r: jnp.where(selk_ref[qt - 1] > 0.5, s + bias_ref[r, 1], NEG))

    def slc_far(kt, carry):
        tile(ks_ref, vs_ref, kt, lambda s, r: jnp.where(selk_ref[kt] > 0.5, s + far_of(r), NEG))
        return carry

    lax.fori_loop(0, jnp.maximum(qt - 1, 0), slc_far, 0)
    for r in range(NSA_GROUP):
        oslc_ref[r] = acc_ref[r] / l_ref[r]

    reset()
    tile(kw_ref, vw_ref, qt, lambda s, r: s + bias_ref[r, 0])

    @pl.when(qt >= 1)
    def _win_left():
        tile(kw_ref, vw_ref, qt - 1, lambda s, r: s + bias_ref[r, 1])

    @pl.when(qt >= 2)
    def _win_edge():
        row = lax.broadcasted_iota(jnp.int32, (t, t), 0)
        col = lax.broadcasted_iota(jnp.int32, (t, t), 1)
        tile(kw_ref, vw_ref, qt - 2, lambda s, r: jnp.where(col > row, s + far_of(r), NEG))

    gate = jax.nn.sigmoid(g_ref[...])
    for r in range(NSA_GROUP):
        o = (gate[:, 3 * r:3 * r + 1] * ocmp_ref[:, r * HEAD_DIM:(r + 1) * HEAD_DIM]
             + gate[:, 3 * r + 1:3 * r + 2] * oslc_ref[r]
             + gate[:, 3 * r + 2:3 * r + 3] * (acc_ref[r] / l_ref[r]))
        o_ref[:, r * HEAD_DIM:(r + 1) * HEAD_DIM] = o.astype(o_ref.dtype)


def nsa_prompt(proj, sel, o_cmp, gates, tiles, far):
    b, s, _ = proj.shape
    t = ATT_TILE
    assert 2 * t == NSA_WINDOW
    nkt = s // t
    gw = NSA_GROUP * HEAD_DIM
    kv0 = (2 * DIFF_HEADS * 2 * DIFF_DK + DIFF_HEADS * HEAD_DIM + NSA_HEADS * HEAD_DIM) // HEAD_DIM + 2 * NSA_KV_HEADS
    expand = jnp.asarray(np.arange(LANES)[:, None] == np.arange(s)[None, :] // NSA_SEL_BLOCK, BF16)
    kv = lambda off: pl.BlockSpec((None, s, HEAD_DIM), lambda bi, g, qt: (bi, 0, kv0 + off + g))
    return pl.pallas_call(
        _nsa_prompt_kernel,
        grid=(b, NSA_KV_HEADS, nkt),
        in_specs=[pl.BlockSpec((None, t, gw), lambda bi, g, qt: (bi, qt, 6 + g)),
                  kv(0), kv(2), kv(4), kv(6),
                  pl.BlockSpec((None, None, t, LANES), lambda bi, g, qt: (bi, g, qt, 0)),
                  pl.BlockSpec((None, t, gw), lambda bi, g, qt: (bi, qt, g)),
                  pl.BlockSpec((None, t, LANES), lambda bi, g, qt: (bi, qt, g)),
                  pl.BlockSpec((None, NSA_GROUP, 2, t, t), lambda bi, g, qt: (g, 0, 0, 0, 0)),
                  pl.BlockSpec((None, NSA_GROUP, LANES), lambda bi, g, qt: (g, 0, 0)),
                  pl.BlockSpec((LANES, s), lambda bi, g, qt: (0, 0))],
        out_specs=pl.BlockSpec((None, t, gw), lambda bi, g, qt: (bi, qt, g)),
        out_shape=jax.ShapeDtypeStruct((b, s, NSA_KV_HEADS * gw), BF16),
        scratch_shapes=[pltpu.VMEM((nkt, t, t), F32), pltpu.VMEM((NSA_GROUP, t, HEAD_DIM), F32),
                        pltpu.VMEM((NSA_GROUP, t, 1), F32), pltpu.VMEM((NSA_GROUP, t, 1), F32),
                        pltpu.VMEM((NSA_GROUP, t, HEAD_DIM), F32)],
        compiler_params=_params(3),
        name="nsa_prompt",
    )(proj, proj, proj, proj, proj, sel, o_cmp, gates,
      tiles.reshape(NSA_KV_HEADS, NSA_GROUP, 2, t, t), far.reshape(NSA_KV_HEADS, NSA_GROUP, LANES), expand)


def _moba_prompt_kernel(q_ref, k_ref, v_ref, bias_ref, far_ref, o_ref, kmean_ref, m_ref, l_ref, acc_ref):
    t = q_ref.shape[0]
    qt = pl.program_id(2)
    nb = k_ref.shape[0] // t
    scale = HEAD_DIM ** -0.5

    @pl.when(qt == 0)
    def _means():
        kmean_ref[...] = jnp.zeros_like(kmean_ref)
        for j in range(nb):
            kmean_ref[j:j + 1, :] = jnp.mean(k_ref[j * t:(j + 1) * t, :], axis=0, keepdims=True)

    q = q_ref[...].astype(BF16)
    lane = lax.broadcasted_iota(jnp.int32, (t, LANES), 1)
    score = jnp.where(lane < qt, _dot_nt(q, kmean_ref[...].astype(BF16)), -jnp.inf)
    sel = ((lane < qt) & _rank_select(score, lane, nb, MOBA_TOPK)).astype(F32)
    _flash_init(m_ref, l_ref, acc_ref)

    def tile(kt, bias, masked):
        start = pl.multiple_of(kt * t, t)
        kb = k_ref[pl.ds(start, t), :].astype(BF16)
        vb = v_ref[pl.ds(start, t), :].astype(BF16)
        s = _dot_nt(q, kb) * scale + bias
        if masked:
            keep = jnp.sum(jnp.where(lane == kt, sel, 0.0), -1, keepdims=True)
            s = jnp.where(keep > 0.5, s, NEG)
        _flash_step(s, vb, m_ref, l_ref, acc_ref)

    tile(qt, bias_ref[0], False)

    @pl.when(qt >= 1)
    def _left():
        tile(qt - 1, bias_ref[1], True)

    far = far_ref[...][:, :1]

    def far_body(kt, carry):
        tile(kt, far, True)
        return carry

    lax.fori_loop(0, jnp.maximum(qt - 1, 0), far_body, 0)
    o_ref[...] = (acc_ref[...] / l_ref[...]).astype(o_ref.dtype)


def moba_prompt(proj, tiles, far):
    b, s, _ = proj.shape
    t = ATT_TILE
    assert t == MOBA_BLOCK and s // t <= LANES
    return pl.pallas_call(
        _moba_prompt_kernel,
        grid=(b, MOBA_HEADS, s // t),
        in_specs=[pl.BlockSpec((None, t, HEAD_DIM), lambda bi, h, qt: (bi, qt, h)),
                  pl.BlockSpec((None, s, HEAD_DIM), lambda bi, h, qt: (bi, 0, MOBA_HEADS + h)),
                  pl.BlockSpec((None, s, HEAD_DIM), lambda bi, h, qt: (bi, 0, 2 * MOBA_HEADS + h)),
                  pl.BlockSpec((None, 2, t, t), lambda bi, h, qt: (h, 0, 0, 0)),
                  pl.BlockSpec((None, 1, LANES), lambda bi, h, qt: (h, 0, 0))],
        out_specs=pl.BlockSpec((None, t, HEAD_DIM), lambda bi, h, qt: (bi, qt, h)),
        out_shape=jax.ShapeDtypeStruct((b, s, MOBA_HEADS * HEAD_DIM), BF16),
        scratch_shapes=[pltpu.VMEM((LANES, HEAD_DIM), F32), pltpu.VMEM((t, 1), F32), pltpu.VMEM((t, 1), F32),
                        pltpu.VMEM((t, HEAD_DIM), F32)],
        compiler_params=_params(3),
        name="moba_prompt",
    )(proj, proj, proj, tiles, far)


def _decode_bias(tab, row_t, row_col, row_kv, n_kv, qpos0, kpos, window=None):
    c = kpos.shape[0] * n_kv
    col_kv = np.arange(c) % n_kv
    kp = np.repeat(kpos, n_kv)
    dist = (qpos0 + row_t)[:, None] - kp[None, :]
    ok = (row_kv[:, None] == col_kv[None, :]) & (dist >= 0) & (kp >= 0)[None, :]
    if window is not None:
        ok = ok & (dist < window)
    b = tab[_t5_bucket(jnp.asarray(dist)), jnp.asarray(row_col)[:, None]].astype(F32)
    return jnp.where(jnp.asarray(ok), b, NEG)


def _decode_kernel(*refs, scale, ppc, n_pages, masked, diff_lam_init):
    pt_ref, q_ref, kn_ref, vn_ref, bnew_ref, tiles_ref = refs[:6]
    pos = 6
    rm_ref = None
    if masked:
        rm_ref = refs[pos]
        pos += 1
    lam_ref = sub_ref = None
    if diff_lam_init is not None:
        lam_ref, sub_ref = refs[pos:pos + 2]
        pos += 2
    k_refs = refs[pos:pos + ppc]
    v_refs = refs[pos + ppc:pos + 2 * ppc]
    o_ref, m_ref, l_ref, acc_ref = refs[pos + 2 * ppc:]
    c = pl.program_id(1)
    q = q_ref[...].astype(BF16)
    rows = q.shape[0]
    cols = k_refs[0].shape[0]

    @pl.when(c == 0)
    def _new_tokens():
        _flash_init(m_ref, l_ref, acc_ref)
        s = _dot_nt(q, kn_ref[...].astype(BF16)) * scale + bnew_ref[...]
        _flash_step(s, vn_ref[...].astype(BF16), m_ref, l_ref, acc_ref)

    if masked:
        lane = lax.broadcasted_iota(jnp.int32, (rows, LANES), 1)
        col = lax.broadcasted_iota(jnp.int32, (rows, cols), 1)
        rm = rm_ref[...]
    for p in range(ppc):
        gp = c * ppc + p
        tile = tiles_ref[1]
        if p == 0:
            tile = jnp.where(gp == 0, tiles_ref[0], tile)
        if p == ppc - 1:
            tile = jnp.where(gp == n_pages - 1, tiles_ref[2], tile)
        s = _dot_nt(q, k_refs[p][...].astype(BF16)) * scale + tile
        if masked:
            lo = jnp.sum(jnp.where(lane == 2 * gp, rm, 0.0), -1, keepdims=True)
            hi = jnp.sum(jnp.where(lane == 2 * gp + 1, rm, 0.0), -1, keepdims=True)
            s = jnp.where(jnp.where(col < cols // 2, lo, hi) > 0.5, s, NEG)
        _flash_step(s, v_refs[p][...].astype(BF16), m_ref, l_ref, acc_ref)

    @pl.when(c == pl.num_programs(1) - 1)
    def _finish():
        o = acc_ref[...] / l_ref[...]
        if diff_lam_init is not None:
            lam = lam_ref[...]
            lam_full = (jnp.exp(jnp.sum(lam[0:1] * lam[1:2], keepdims=True))
                        - jnp.exp(jnp.sum(lam[2:3] * lam[3:4], keepdims=True)) + diff_lam_init)
            half = rows // 2
            o = o[:half] - lam_full * o[half:]
            o = o * lax.rsqrt(jnp.mean(o * o, -1, keepdims=True) + RMS_EPS) * sub_ref[...] * (1.0 - diff_lam_init)
        o_ref[...] = o.astype(o_ref.dtype)


def decode_attn(page_table, q, k_new, v_new, bias_new, tiles, k_pages, v_pages, n_chunks, scale, out_dtype,
                row_mask=None, diff=None, name="decode_attn"):
    bs, r, _ = q.shape
    n_pages = page_table.shape[1]
    ppc = n_pages // n_chunks
    c = k_pages.shape[1]
    per_b = lambda shape: pl.BlockSpec((None,) + shape, lambda b, ch, pt: (b, 0, 0))
    const = lambda shape: pl.BlockSpec(shape, lambda b, ch, pt: (0,) * len(shape))
    in_specs = [per_b((r, HEAD_DIM)), per_b((LANES, HEAD_DIM)), per_b((LANES, HEAD_DIM)),
                const((r, LANES)), const((3, r, c))]
    args = [q, k_new, v_new, bias_new, tiles]
    if row_mask is not None:
        in_specs.append(per_b((r, LANES)))
        args.append(row_mask)
    r_out = r
    if diff is not None:
        in_specs += [const(diff[0].shape), const((1, HEAD_DIM))]
        args += [diff[0], diff[1].reshape(1, HEAD_DIM)]
        r_out = r // 2
    page = lambda p: pl.BlockSpec((None, c, HEAD_DIM), lambda b, ch, pt: (pt[b, ch * ppc + p], 0, 0))
    in_specs += [page(p) for p in range(ppc)] * 2
    args += [k_pages] * ppc + [v_pages] * ppc
    kern = functools.partial(_decode_kernel, scale=scale, ppc=ppc, n_pages=n_pages, masked=row_mask is not None,
                             diff_lam_init=None if diff is None else diff[2])
    grid_spec = pltpu.PrefetchScalarGridSpec(
        num_scalar_prefetch=1,
        grid=(bs, n_chunks),
        in_specs=in_specs,
        out_specs=pl.BlockSpec((None, r_out, HEAD_DIM), lambda b, ch, pt: (b, 0, 0)),
        scratch_shapes=[pltpu.VMEM((r, 1), F32), pltpu.VMEM((r, 1), F32), pltpu.VMEM((r, HEAD_DIM), F32)],
    )
    return pl.pallas_call(
        kern,
        grid_spec=grid_spec,
        out_shape=jax.ShapeDtypeStruct((bs, r_out, HEAD_DIM), out_dtype),
        compiler_params=_params(2),
        name=name,
    )(page_table, *args)


def _pad_rows(x, n):
    return jnp.pad(x, ((0, 0), (0, n - x.shape[1]), (0, 0)))


def _moba_decode_kernel(pt_ref, q_ref, kn_ref, vn_ref, bnew_ref, tiles_ref, k0_ref, k1_ref, v0_ref, v1_ref, o_ref,
                        ssel_ref, mb_ref, lb_ref, ab_ref, m_ref, l_ref, acc_ref):
    j = pl.program_id(1)
    nb = pl.num_programs(1)
    n_blk = mb_ref.shape[0]
    q = q_ref[...].astype(BF16)
    rows = q.shape[0]
    scale = HEAD_DIM ** -0.5
    lane = lax.broadcasted_iota(jnp.int32, (rows, LANES), 1)

    @pl.when(j == 0)
    def _init():
        ssel_ref[...] = jnp.full_like(ssel_ref, -jnp.inf)

    _flash_init(m_ref, l_ref, acc_ref)
    ksum = jnp.zeros((MOBA_HEADS, HEAD_DIM), F32)
    for p, (k_ref, v_ref) in enumerate(((k0_ref, v0_ref), (k1_ref, v1_ref))):
        kp = k_ref[...]
        ksum = ksum + jnp.sum(kp.reshape(PAGE_SIZE, MOBA_HEADS, HEAD_DIM), axis=0)
        tile = tiles_ref[0]
        if p == 1:
            tile = jnp.where(j == nb - 1, tiles_ref[1], tile)
        s = _dot_nt(q, kp.astype(BF16)) * scale + tile
        _flash_step(s, v_ref[...].astype(BF16), m_ref, l_ref, acc_ref)
    mb_ref[j] = m_ref[...]
    lb_ref[j] = l_ref[...]
    ab_ref[j] = acc_ref[...]
    kmean = (ksum * (1.0 / MOBA_BLOCK)).astype(BF16).astype(F32)
    kmean_rows = jnp.concatenate([kmean] * (rows // MOBA_HEADS), axis=0)
    score = jnp.sum(q.astype(F32) * kmean_rows, -1, keepdims=True)
    ssel_ref[...] = jnp.where(lane == j, score, ssel_ref[...])

    @pl.when(j == nb - 1)
    def _finish():
        chosen = (lane < n_blk) & _rank_select(ssel_ref[...], lane, n_blk, MOBA_TOPK)
        sel = chosen.astype(F32)
        s_new = _dot_nt(q, kn_ref[...].astype(BF16)) * scale + bnew_ref[...]
        keep = [jnp.sum(jnp.where(lane == b, sel, 0.0), -1, keepdims=True) > 0.5 for b in range(n_blk)]
        m_tot = jnp.max(s_new, -1, keepdims=True)
        for b in range(n_blk):
            m_tot = jnp.maximum(m_tot, jnp.where(keep[b], mb_ref[b], NEG))
        p_new = jnp.exp(s_new - m_tot)
        l_tot = jnp.sum(p_new, -1, keepdims=True)
        acc = _dot(p_new.astype(BF16), vn_ref[...].astype(BF16))
        for b in range(n_blk):
            w = jnp.where(keep[b], jnp.exp(mb_ref[b] - m_tot), 0.0)
            l_tot = l_tot + w * lb_ref[b]
            acc = acc + w * ab_ref[b]
        o_ref[...] = (acc / l_tot).astype(o_ref.dtype)


def moba_decode(page_table, q, k_new, v_new, bias_new, tiles, k_pages, v_pages):
    bs, r, _ = q.shape
    n_pages = page_table.shape[1]
    ppb = MOBA_BLOCK // PAGE_SIZE
    assert ppb == 2 and n_pages % ppb == 0 and n_pages // ppb >= MOBA_TOPK
    n_blk = n_pages // ppb
    c = k_pages.shape[1]
    per_b = lambda shape: pl.BlockSpec((None,) + shape, lambda b, j, pt: (b, 0, 0))
    const = lambda shape: pl.BlockSpec(shape, lambda b, j, pt: (0,) * len(shape))
    page = lambda p: pl.BlockSpec((None, c, HEAD_DIM), lambda b, j, pt: (pt[b, ppb * j + p], 0, 0))
    grid_spec = pltpu.PrefetchScalarGridSpec(
        num_scalar_prefetch=1,
        grid=(bs, n_blk),
        in_specs=[per_b((r, HEAD_DIM)), per_b((LANES, HEAD_DIM)), per_b((LANES, HEAD_DIM)),
                  const((r, LANES)), const((2, r, c)), page(0), page(1), page(0), page(1)],
        out_specs=pl.BlockSpec((None, r, HEAD_DIM), lambda b, j, pt: (b, 0, 0)),
        scratch_shapes=[pltpu.VMEM((r, LANES), F32), pltpu.VMEM((n_blk, r, 1), F32), pltpu.VMEM((n_blk, r, 1), F32),
                        pltpu.VMEM((n_blk, r, HEAD_DIM), F32),
                        pltpu.VMEM((r, 1), F32), pltpu.VMEM((r, 1), F32), pltpu.VMEM((r, HEAD_DIM), F32)],
    )
    return pl.pallas_call(
        _moba_decode_kernel,
        grid_spec=grid_spec,
        out_shape=jax.ShapeDtypeStruct((bs, r, HEAD_DIM), BF16),
        compiler_params=_params(2),
        name="moba_decode",
    )(page_table, q, k_new, v_new, bias_new, tiles, k_pages, k_pages, v_pages, v_pages)


def _nsa_combine_kernel(oc_ref, os_ref, ow_ref, g_ref, o_ref):
    gate = jax.nn.sigmoid(g_ref[...])
    o = gate[..., 0:1] * oc_ref[...] + gate[..., 1:2] * os_ref[...] + gate[..., 2:3] * ow_ref[...]
    o_ref[...] = o.astype(o_ref.dtype)


def nsa_combine(o_cmp, o_slc, o_win, gates):
    full = lambda a: pl.BlockSpec(a.shape, lambda: (0,) * a.ndim)
    return pl.pallas_call(
        _nsa_combine_kernel,
        in_specs=[full(o_cmp), full(o_slc), full(o_win), full(gates)],
        out_specs=full(o_cmp),
        out_shape=jax.ShapeDtypeStruct(o_cmp.shape, BF16),
        compiler_params=pltpu.CompilerParams(vmem_limit_bytes=VMEM_LIMIT),
        name="nsa_combine",
    )(o_cmp, o_slc, o_win, gates)


AB_MAIN = 2 * DIFF_HEADS * 2 * DIFF_DK + DIFF_HEADS * HEAD_DIM + NSA_HEADS * HEAD_DIM + 6 * NSA_KV_HEADS * HEAD_DIM
ROW_TILE = 512


def _in_proj(h, w, tn):
    m = h.shape[0]
    n = w.shape[1]
    n_main = n // tn * tn
    tm = min(m, ROW_TILE)
    main = matmul(h, w.astype(BF16), n_main, tm, tn, "in_proj")
    if n_main == n:
        return main, None
    tail = jnp.pad(w[:, n_main:], ((0, 0), (0, LANES - (n - n_main)))).astype(BF16)
    return main, matmul(h, tail, LANES, tm, LANES, "in_proj_tail")[:, :n - n_main]


def _even_prompt(proj, gates, b, s, lam, subln, lam_init, pe, w1, w2, tab, tiles, far):
    p3 = proj.reshape(b, s, AB_MAIN)
    o_diff = diff_prompt(p3, tiles[:DIFF_HEADS], far[:DIFF_HEADS], lam, subln, lam_init)
    kv = NSA_KV_HEADS * HEAD_DIM
    c0 = AB_MAIN - 6 * kv
    kc, vc, ks, vs, kw, vw = [p3[..., c0 + i * kv:c0 + (i + 1) * kv] for i in range(6)]
    tab_n = tab[:, DIFF_HEADS:]
    kcmp = nsa_compress(kc, pe[0], w1[0], w2[0], None)
    vcmp = nsa_compress(vc, pe[1], w1[1], w2[1], None)
    o_cmp, sel = nsa_cmp_select(p3, 6, ATT_TILE, kcmp, vcmp, tab_n, 0, s)
    o_nsa = nsa_prompt(p3, sel, o_cmp, _gate_pad(gates).reshape(b, s, NSA_KV_HEADS * LANES),
                       tiles[DIFF_HEADS:], far[DIFF_HEADS:])
    hd = DIFF_HEADS * HEAD_DIM
    keep = min(NSA_WINDOW, s)
    state = (p3[..., hd:2 * hd].reshape(b, s, DIFF_HEADS, HEAD_DIM),
             p3[..., 2 * hd:3 * hd].reshape(b, s, DIFF_HEADS, HEAD_DIM),
             *[a.reshape(b, s, NSA_KV_HEADS, HEAD_DIM) for a in (kc, vc, ks, vs)],
             kw[:, s - keep:].reshape(b, keep, NSA_KV_HEADS, HEAD_DIM),
             vw[:, s - keep:].reshape(b, keep, NSA_KV_HEADS, HEAD_DIM))
    return o_diff.reshape(b * s, hd), o_nsa.reshape(b * s, NSA_HEADS * HEAD_DIM), state


def _even_sample(proj, gates, bs, ts, page_table, cache_dk, cache_dv, cache_ck, cache_cv, cache_sk, cache_sv,
                 win_k, win_v, lam, subln, lam_init, pe, w1, w2, tab):
    n_pool = cache_dk.shape[0]
    n_pages = page_table.shape[1]
    past = n_pages * PAGE_SIZE
    hd = DIFF_HEADS * HEAD_DIM
    kv = NSA_KV_HEADS * HEAD_DIM
    p3 = proj.reshape(bs, ts, AB_MAIN)
    dq, dk, dv, nq = [p3[..., i * hd:(i + 1) * hd] for i in range(4)]
    c0 = AB_MAIN - 6 * kv
    kc, vc, ks, vs, kw, vw = [p3[..., c0 + i * kv:c0 + (i + 1) * kv] for i in range(6)]
    tab_d, tab_n = tab[:, :DIFF_HEADS], tab[:, DIFF_HEADS:]
    new_pos = np.where(np.arange(LANES) < ts, past + np.arange(LANES), -1)

    r = ts * DIFF_HEADS
    q = dq.reshape(bs, r, HEAD_DIM)
    lane = jnp.arange(HEAD_DIM)
    q2 = jnp.concatenate([jnp.where(lane < DIFF_DK, q, 0.0), jnp.where(lane >= DIFF_DK, q, 0.0)], axis=1)
    row_t = np.tile(np.repeat(np.arange(ts), DIFF_HEADS), 2)
    row_h = np.tile(np.arange(DIFF_HEADS), 2 * ts)
    page_pos = lambda p: p * PAGE_SIZE + np.arange(PAGE_SIZE)
    d_bias = lambda kpos: _decode_bias(tab_d, row_t, row_h, row_h, DIFF_HEADS, past, kpos)
    tiles_d = jnp.stack([d_bias(page_pos(p)) for p in (0, 1, n_pages - 1)])
    o_diff = decode_attn(page_table, q2, _pad_rows(dk.reshape(bs, r, HEAD_DIM), LANES),
                         _pad_rows(dv.reshape(bs, r, HEAD_DIM), LANES), d_bias(new_pos[:LANES // DIFF_HEADS]), tiles_d,
                         cache_dk.reshape(n_pool, PAGE_SIZE * DIFF_HEADS, HEAD_DIM),
                         cache_dv.reshape(n_pool, PAGE_SIZE * DIFF_HEADS, HEAD_DIM),
                         2, DIFF_DK ** -0.5, BF16, diff=(lam, subln, lam_init), name="diff_decode")

    rn = ts * NSA_HEADS
    qn = nq.reshape(bs, rn, HEAD_DIM)
    row_tn = np.repeat(np.arange(ts), NSA_HEADS)
    row_cn = np.tile(np.arange(NSA_HEADS), ts)
    row_gn = row_cn // NSA_GROUP
    kcmp = nsa_compress(cache_ck.reshape(n_pool, PAGE_SIZE, kv), pe[0], w1[0], w2[0], page_table)
    vcmp = nsa_compress(cache_cv.reshape(n_pool, PAGE_SIZE, kv), pe[1], w1[1], w2[1], page_table)
    o_cmp, sel = nsa_cmp_select(p3, 6, ts, kcmp, vcmp, tab_n, past, past + ts)
    row_mask = jnp.broadcast_to(jnp.swapaxes(sel, 1, 2)[:, :, :, None, :], (bs, ts, NSA_KV_HEADS, NSA_GROUP, LANES))
    n_bias = lambda kpos, window=None: _decode_bias(tab_n, row_tn, row_cn, row_gn, NSA_KV_HEADS, past, kpos, window)
    new_n = new_pos[:LANES // NSA_KV_HEADS]
    k_new = lambda a: _pad_rows(a.reshape(bs, ts * NSA_KV_HEADS, HEAD_DIM), LANES)
    tiles_s = jnp.stack([n_bias(page_pos(p)) for p in (0, 1, n_pages - 1)])
    o_slc = decode_attn(page_table, qn, k_new(ks), k_new(vs), n_bias(new_n), tiles_s,
                        cache_sk.reshape(n_pool, PAGE_SIZE * NSA_KV_HEADS, HEAD_DIM),
                        cache_sv.reshape(n_pool, PAGE_SIZE * NSA_KV_HEADS, HEAD_DIM),
                        1, HEAD_DIM ** -0.5, F32, row_mask=row_mask.reshape(bs, rn, LANES), name="nsa_slc_decode")
    w_buf = win_k.shape[1]
    w_pages = w_buf // PAGE_SIZE
    win_table = jnp.arange(bs * w_pages, dtype=jnp.int32).reshape(bs, w_pages)
    win_pos = lambda p: past - w_buf + p * PAGE_SIZE + np.arange(PAGE_SIZE)
    tiles_w = jnp.stack([n_bias(win_pos(p), NSA_WINDOW) for p in (0, 1, w_pages - 1)])
    o_win = decode_attn(win_table, qn, k_new(kw), k_new(vw), n_bias(new_n, NSA_WINDOW), tiles_w,
                        win_k.reshape(bs * w_pages, PAGE_SIZE * NSA_KV_HEADS, HEAD_DIM),
                        win_v.reshape(bs * w_pages, PAGE_SIZE * NSA_KV_HEADS, HEAD_DIM),
                        1, HEAD_DIM ** -0.5, F32, name="nsa_win_decode")
    o_nsa = nsa_combine(o_cmp.reshape(bs, rn, HEAD_DIM), o_slc, o_win, gates.reshape(bs, rn, 3))

    as_kv = lambda a: a.reshape(bs, ts, NSA_KV_HEADS, HEAD_DIM)
    keep = min(NSA_WINDOW, past + ts)
    state = (dk.reshape(bs, ts, DIFF_HEADS, HEAD_DIM), dv.reshape(bs, ts, DIFF_HEADS, HEAD_DIM),
             as_kv(kc), as_kv(vc), as_kv(ks), as_kv(vs),
             jnp.concatenate([win_k, as_kv(kw)], axis=1)[:, w_buf + ts - keep:],
             jnp.concatenate([win_v, as_kv(vw)], axis=1)[:, w_buf + ts - keep:])
    return o_diff.reshape(bs * ts, hd), o_nsa.reshape(bs * ts, NSA_HEADS * HEAD_DIM), state


def _odd_sample(proj, bs, ts, page_table, cache_k, cache_v, tab):
    n_pool = cache_k.shape[0]
    n_pages = page_table.shape[1]
    past = n_pages * PAGE_SIZE
    hd = MOBA_HEADS * HEAD_DIM
    p3 = proj.reshape(bs, ts, 3 * hd)
    q, k, v = [p3[..., i * hd:(i + 1) * hd] for i in range(3)]
    r = ts * MOBA_HEADS
    row_t = np.repeat(np.arange(ts), MOBA_HEADS)
    row_h = np.tile(np.arange(MOBA_HEADS), ts)
    bias = lambda kpos: _decode_bias(tab, row_t, row_h, row_h, MOBA_HEADS, past, kpos)
    page_pos = lambda p: p * PAGE_SIZE + np.arange(PAGE_SIZE)
    new_pos = np.where(np.arange(LANES // MOBA_HEADS) < ts, past + np.arange(LANES // MOBA_HEADS), -1)
    tiles = jnp.stack([bias(page_pos(1)), bias(page_pos(n_pages - 1))])
    o = moba_decode(page_table, q.reshape(bs, r, HEAD_DIM), _pad_rows(k.reshape(bs, r, HEAD_DIM), LANES),
                    _pad_rows(v.reshape(bs, r, HEAD_DIM), LANES), bias(new_pos), tiles,
                    cache_k.reshape(n_pool, PAGE_SIZE * MOBA_HEADS, HEAD_DIM),
                    cache_v.reshape(n_pool, PAGE_SIZE * MOBA_HEADS, HEAD_DIM))
    state = (k.reshape(bs, ts, MOBA_HEADS, HEAD_DIM), v.reshape(bs, ts, MOBA_HEADS, HEAD_DIM))
    return o.reshape(bs * ts, hd), state


def kernel(x_prompt, x_sample, c_prompt, c_sample, page_table, cache_diff_k, cache_diff_v, cache_nsa_cmp_k, cache_nsa_cmp_v, cache_nsa_slc_k, cache_nsa_slc_v, state_nsa_win_k, state_nsa_win_v, cache_moba_k, cache_moba_v, w_in_ab, w_out_ab, diff_lambda, diff_subln_g, nsa_cmp_pe, nsa_cmp_w1, nsa_cmp_w2, w_in_c, w_out_c, rel_bias, ada_w, ada_b, ln_g, ln_b, router_group_w, router_group_b, router_expert_w, router_expert_b, moe_w_gate, moe_w_up, moe_w_down):
    b, s, d = x_prompt.shape
    bs, ts, _ = x_sample.shape
    mp, ms = b * s, bs * ts
    depth = ada_w.shape[0]
    mod = ada_mod(jnp.concatenate([c_prompt, c_sample]), ada_w.reshape(2 * depth, d, 3 * d), ada_b.reshape(2 * depth, 3 * d))

    def mod_vectors(sub):
        per_tok = jnp.repeat(mod[sub, b:], ts, axis=0)
        return ([mod[sub, :b, None, i * d:(i + 1) * d] for i in range(3)],
                [per_tok[None, :, i * d:(i + 1) * d] for i in range(3)])

    xp, xs = x_prompt.reshape(mp, d), x_sample.reshape(ms, d)
    tiles, far = _toeplitz_tiles(rel_bias, ATT_TILE), _far_bias(rel_bias)
    even_p, even_s, odd_p, odd_s = [], [], [], []
    for l in range(depth):
        i = l // 2
        (shp, scp, gp), (shs, scs, gs) = mod_vectors(2 * l)
        hp = mod_cast(xp, shp, scp, ROW_TILE)
        hs = mod_cast(xs, shs, scs, ms)
        if l % 2 == 0:
            lam_init = 0.8 - 0.6 * math.exp(-0.3 * l)
            w1, w2 = nsa_cmp_w1[i].astype(BF16), nsa_cmp_w2[i].astype(BF16)
            proj_p, gates_p = _in_proj(hp, w_in_ab[i], 512)
            proj_s, gates_s = _in_proj(hs, w_in_ab[i], 512)
            oa_p, ob_p, st_p = _even_prompt(proj_p, gates_p, b, s, diff_lambda[i], diff_subln_g[i], lam_init,
                                            nsa_cmp_pe[i], w1, w2, rel_bias, tiles, far)
            oa_s, ob_s, st_s = _even_sample(proj_s, gates_s, bs, ts, page_table, cache_diff_k[i], cache_diff_v[i],
                                            cache_nsa_cmp_k[i], cache_nsa_cmp_v[i], cache_nsa_slc_k[i],
                                            cache_nsa_slc_v[i], state_nsa_win_k[i], state_nsa_win_v[i],
                                            diff_lambda[i], diff_subln_g[i], lam_init, nsa_cmp_pe[i], w1, w2, rel_bias)
            even_p.append(st_p)
            even_s.append(st_s)
            w_out = w_out_ab[i].astype(BF16)
            cols = (0, 0)
        else:
            proj_p, _ = _in_proj(hp, w_in_c[i], 1024)
            proj_s, _ = _in_proj(hs, w_in_c[i], 1024)
            hd = MOBA_HEADS * HEAD_DIM
            p3 = proj_p.reshape(b, s, 3 * hd)
            oa_p = ob_p = moba_prompt(p3, tiles, far).reshape(mp, hd)
            odd_p.append((p3[..., hd:2 * hd].reshape(b, s, MOBA_HEADS, HEAD_DIM),
                          p3[..., 2 * hd:].reshape(b, s, MOBA_HEADS, HEAD_DIM)))
            oa_s, st_s = _odd_sample(proj_s, bs, ts, page_table, cache_moba_k[i], cache_moba_v[i], rel_bias)
            ob_s = oa_s
            odd_s.append(st_s)
            w_out = w_out_c[i].astype(BF16)
            cols = (0, 1)
        xp = out_proj_norm(oa_p, ob_p, cols[0], cols[1], w_out, xp, gp, ln_g[l, 0], ln_b[l, 0], 256)
        xs = out_proj_norm(oa_s, ob_s, cols[0], cols[1], w_out, xs, gs, ln_g[l, 0], ln_b[l, 0], ms)

        (shp, scp, gp), (shs, scs, gs) = mod_vectors(2 * l + 1)
        hp = mod_cast(xp, shp, scp, ROW_TILE)
        hs = mod_cast(xs, shs, scs, ms)
        rw, rb = _router_pack(router_group_w[l], router_group_b[l], router_expert_w[l], router_expert_b[l])
        wg, wu, wd = moe_w_gate[l].astype(BF16), moe_w_up[l].astype(BF16), moe_w_down[l].astype(BF16)
        xp = moe_norm(hp, rw, rb, wg, wu, wd, xp, gp, ln_g[l, 1], ln_b[l, 1], ROW_TILE)
        xs = moe_norm(hs, rw, rb, wg, wu, wd, xs, gs, ln_g[l, 1], ln_b[l, 1], ms)

    stack = lambda rows, j: jnp.stack([r[j] for r in rows])
    outs = [xp.reshape(b, s, d), xs.reshape(bs, ts, d)]
    for j in range(8):
        outs += [stack(even_p, j), stack(even_s, j)]
    for j in range(2):
        outs += [stack(odd_p, j), stack(odd_s, j)]
    return tuple(outs)
```

```python
import functools
import math

import numpy as np
import jax
import jax.numpy as jnp
from jax import lax
from jax.experimental import pallas as pl
from jax.experimental.pallas import tpu as pltpu

F32 = jnp.float32
BF16 = jnp.bfloat16
NEG = -1e30

HEAD_DIM = 128
DIFF_HEADS = 8
DIFF_DK = 64
NSA_HEADS = 8
NSA_KV_HEADS = 2
NSA_GROUP = 4
NSA_CMP_LEN = 32
NSA_CMP_STRIDE = 16
NSA_CMP_HIDDEN = 256
NSA_SEL_BLOCK = 64
NSA_SEL_TOPK = 16
NSA_WINDOW = 512
MOBA_HEADS = 16
MOBA_BLOCK = 256
MOBA_TOPK = 3
PAGE_SIZE = 128
REL_BUCKETS = 32
REL_MAX_DIST = 128
N_GROUPS = 4
EXPERTS_PER_GROUP = 4
N_EXPERTS = 16
DEPTH = 2
DEEPNORM_ALPHA = (2 * DEPTH) ** 0.25
LN_EPS = 1e-5
RMS_EPS = 1e-6

LANES = 128
ATT_TILE = 256
VMEM_LIMIT = 56 * 1024 * 1024


def _params(n_axes, vmem=VMEM_LIMIT):
    return pltpu.CompilerParams(dimension_semantics=("arbitrary",) * n_axes, vmem_limit_bytes=vmem)


def _dot(a, b):
    return jnp.dot(a, b, preferred_element_type=F32)


def _dot_nt(a, b):
    return lax.dot_general(a, b, (((1,), (1,)), ((), ())), preferred_element_type=F32)


def _ada_kernel(c_ref, w_ref, b_ref, o_ref):
    c = c_ref[...]
    a = (c * jax.nn.sigmoid(c)).astype(BF16)
    o_ref[...] = _dot(a, w_ref[...].astype(BF16)) + b_ref[...]


def ada_mod(c_all, ada_w, ada_b):
    n_sub, d, d3 = ada_w.shape
    bc = c_all.shape[0]
    tn = 768
    return pl.pallas_call(
        _ada_kernel,
        grid=(n_sub, d3 // tn),
        in_specs=[pl.BlockSpec((bc, d), lambda s, j: (0, 0)),
                  pl.BlockSpec((None, d, tn), lambda s, j: (s, 0, j)),
                  pl.BlockSpec((None, 1, tn), lambda s, j: (s, 0, j))],
        out_specs=pl.BlockSpec((None, bc, tn), lambda s, j: (s, 0, j)),
        out_shape=jax.ShapeDtypeStruct((n_sub, bc, d3), F32),
        compiler_params=_params(2),
        name="ada_mod",
    )(c_all, ada_w, ada_b.reshape(n_sub, 1, d3))


def _seg_spec(vec, m, tm):
    g, tg, d = vec.shape
    if tg == 1:
        per = m // g // tm
        return pl.BlockSpec((None, 1, d), lambda i, *_: (i // per, 0, 0))
    return pl.BlockSpec((None, tm, d), lambda i, *_: (0, i, 0))


def _mod_kernel(x_ref, sh_ref, sc_ref, o_ref):
    o_ref[...] = (x_ref[...] * (1 + sc_ref[...]) + sh_ref[...]).astype(BF16)


def mod_cast(x, shift, scale, tm):
    m, d = x.shape
    return pl.pallas_call(
        _mod_kernel,
        grid=(m // tm,),
        in_specs=[pl.BlockSpec((tm, d), lambda i: (i, 0)), _seg_spec(shift, m, tm), _seg_spec(scale, m, tm)],
        out_specs=pl.BlockSpec((tm, d), lambda i: (i, 0)),
        out_shape=jax.ShapeDtypeStruct((m, d), BF16),
        compiler_params=_params(1),
        name="mod_cast",
    )(x, shift, scale)


def _mm_kernel(a_ref, w_ref, o_ref):
    o_ref[...] = _dot(a_ref[...], w_ref[...]).astype(o_ref.dtype)


def matmul(a, w, n_cols, tm, tn, name):
    m, k = a.shape
    return pl.pallas_call(
        _mm_kernel,
        grid=(n_cols // tn, m // tm),
        in_specs=[pl.BlockSpec((tm, k), lambda j, i: (i, 0)), pl.BlockSpec((k, tn), lambda j, i: (0, j))],
        out_specs=pl.BlockSpec((tm, tn), lambda j, i: (i, j)),
        out_shape=jax.ShapeDtypeStruct((m, n_cols), F32),
        compiler_params=_params(2),
        name=name,
    )(a, w)


def _postnorm(x, y, gate, g, b):
    z = DEEPNORM_ALPHA * x + (1 + gate) * y
    zc = z - jnp.mean(z, -1, keepdims=True)
    var = jnp.mean(zc * zc, -1, keepdims=True)
    return zc * lax.rsqrt(var + LN_EPS) * g + b


def _outln_kernel(oa_ref, ob_ref, w_ref, x_ref, gate_ref, g_ref, b_ref, out_ref):
    half = oa_ref.shape[1]
    y = _dot(oa_ref[...], w_ref[:half, :]) + _dot(ob_ref[...], w_ref[half:, :])
    out_ref[...] = _postnorm(x_ref[...], y, gate_ref[...], g_ref[...], b_ref[...])


def out_proj_norm(oa, ob, cols_a, cols_b, w_out, x, gate, ln_g, ln_b, tm):
    m, d = x.shape
    half = w_out.shape[0] // 2
    return pl.pallas_call(
        _outln_kernel,
        grid=(m // tm,),
        in_specs=[pl.BlockSpec((tm, half), lambda i: (i, cols_a)),
                  pl.BlockSpec((tm, half), lambda i: (i, cols_b)),
                  pl.BlockSpec(w_out.shape, lambda i: (0, 0)),
                  pl.BlockSpec((tm, d), lambda i: (i, 0)),
                  _seg_spec(gate, m, tm),
                  pl.BlockSpec((1, d), lambda i: (0, 0)),
                  pl.BlockSpec((1, d), lambda i: (0, 0))],
        out_specs=pl.BlockSpec((tm, d), lambda i: (i, 0)),
        out_shape=jax.ShapeDtypeStruct((m, d), F32),
        compiler_params=_params(1),
        name="out_proj_norm",
    )(oa, ob, w_out, x, gate, ln_g.reshape(1, d), ln_b.reshape(1, d))


def _first_argmax(v, lane, valid):
    vm = jnp.where(valid, v, -jnp.inf)
    top = jnp.max(vm, -1, keepdims=True)
    idx = jnp.min(jnp.where(valid & (vm == top), lane, 1 << 20), -1, keepdims=True)
    return top, idx


def _moe_kernel(h_ref, rw_ref, rb_ref, wg_ref, wu_ref, wd_ref, x_ref, gate_ref, g_ref, b_ref, out_ref,
                comb_ref, acc_ref):
    e = pl.program_id(1)
    h = h_ref[...]

    @pl.when(e == 0)
    def _route():
        logits = _dot(h, rw_ref[...]) + rb_ref[...]
        lane = lax.broadcasted_iota(jnp.int32, logits.shape, 1)
        is_g = lane < N_GROUPS
        gmax = jnp.max(jnp.where(is_g, logits, -jnp.inf), -1, keepdims=True)
        gex = jnp.where(is_g, jnp.exp(logits - gmax), 0.0)
        pg = gex / jnp.sum(gex, -1, keepdims=True)
        pg_top, g_top = _first_argmax(pg, lane, is_g)
        ex_id = lane - N_GROUPS
        in_grp = (ex_id >= g_top * EXPERTS_PER_GROUP) & (ex_id < (g_top + 1) * EXPERTS_PER_GROUP)
        emax = jnp.max(jnp.where(in_grp, logits, -jnp.inf), -1, keepdims=True)
        eex = jnp.where(in_grp, jnp.exp(logits - emax), 0.0)
        pe = eex / jnp.sum(eex, -1, keepdims=True)
        p1, i1 = _first_argmax(pe, lane, in_grp)
        p2, i2 = _first_argmax(pe, lane, in_grp & (lane != i1))
        tot = p1 + p2
        comb = jnp.where(lane == i1, p1 / tot * pg_top, 0.0) + jnp.where(lane == i2, p2 / tot * pg_top, 0.0)
        comb_ref[...] = comb
        acc_ref[...] = jnp.zeros_like(acc_ref)

    lane = lax.broadcasted_iota(jnp.int32, comb_ref.shape, 1)
    c_e = jnp.sum(jnp.where(lane == e + N_GROUPS, comb_ref[...], 0.0), -1, keepdims=True)
    a = _dot(h, wg_ref[...])
    u = _dot(h, wu_ref[...])
    hid = (a * jax.nn.sigmoid(a)) * u * c_e
    acc_ref[...] += _dot(hid.astype(BF16), wd_ref[...])

    @pl.when(e == pl.num_programs(1) - 1)
    def _finish():
        out_ref[...] = _postnorm(x_ref[...], acc_ref[...], gate_ref[...], g_ref[...], b_ref[...])


def _router_pack(wg_r, bg_r, we_r, be_r):
    d = wg_r.shape[0]
    pad = LANES - N_GROUPS - N_EXPERTS
    rw = jnp.concatenate([wg_r, we_r, jnp.zeros((d, pad), F32)], axis=1).astype(BF16)
    rb = jnp.concatenate([bg_r, be_r, jnp.zeros((pad,), F32)]).reshape(1, LANES)
    return rw, rb


def moe_norm(h, rw, rb, wg, wu, wd, x, gate, ln_g, ln_b, tm):
    m, d = x.shape
    n_e, _, f = wg.shape
    return pl.pallas_call(
        _moe_kernel,
        grid=(m // tm, n_e),
        in_specs=[pl.BlockSpec((tm, d), lambda i, e: (i, 0)),
                  pl.BlockSpec(rw.shape, lambda i, e: (0, 0)),
                  pl.BlockSpec(rb.shape, lambda i, e: (0, 0)),
                  pl.BlockSpec((None, d, f), lambda i, e: (e, 0, 0)),
                  pl.BlockSpec((None, d, f), lambda i, e: (e, 0, 0)),
                  pl.BlockSpec((None, f, d), lambda i, e: (e, 0, 0)),
                  pl.BlockSpec((tm, d), lambda i, e: (i, 0)),
                  _seg_spec(gate, m, tm),
                  pl.BlockSpec((1, d), lambda i, e: (0, 0)),
                  pl.BlockSpec((1, d), lambda i, e: (0, 0))],
        out_specs=pl.BlockSpec((tm, d), lambda i, e: (i, 0)),
        out_shape=jax.ShapeDtypeStruct((m, d), F32),
        scratch_shapes=[pltpu.VMEM((tm, LANES), F32), pltpu.VMEM((tm, d), F32)],
        compiler_params=_params(2),
        name="moe_norm",
    )(h, rw, rb, wg, wu, wd, x, gate, ln_g.reshape(1, d), ln_b.reshape(1, d))


def _t5_bucket(dist):
    n = jnp.maximum(dist, 0)
    max_exact = REL_BUCKETS // 2
    nf = jnp.maximum(n, 1).astype(F32)
    log_b = max_exact + (jnp.log(nf / max_exact) / math.log(REL_MAX_DIST / max_exact) * (REL_BUCKETS - max_exact)).astype(jnp.int32)
    return jnp.where(n < max_exact, n, jnp.minimum(log_b, REL_BUCKETS - 1))


def _lookup(cols, bucket):
    out = jnp.zeros(jnp.broadcast_shapes(cols.shape[1:], bucket.shape), F32)
    for k in range(REL_BUCKETS):
        out = jnp.where(bucket == k, cols[k], out)
    return out


def _bias_of(tab, dist, mask):
    cols = tab.astype(F32).reshape(tab.shape + (1,) * dist.ndim)
    return jnp.where(mask, _lookup(cols, _t5_bucket(dist)), NEG)


def _toeplitz_tiles(tab, t):
    i = jnp.arange(t)[:, None]
    j = jnp.arange(t)[None, :]
    diag = _bias_of(tab, i - j, i >= j)
    left = _bias_of(tab, t + i - j, jnp.ones((t, t), bool))
    return jnp.stack([diag, left], axis=1)


def _far_bias(tab):
    return jnp.broadcast_to(tab[REL_BUCKETS - 1].astype(F32)[:, None, None], (tab.shape[1], 1, LANES))


def _attend_blocks(q, blocks, s_ref, m_floor=None):
    m_el = None
    for i, (load_k, _, logits_fn) in enumerate(blocks):
        s = logits_fn(_dot_nt(q, load_k()))
        s_ref[i] = s
        m_el = s if m_el is None else jnp.maximum(m_el, s)
    m = jnp.max(m_el, -1, keepdims=True)
    if m_floor is not None:
        m = jnp.maximum(m, m_floor)
    p_sum = acc = None
    for i, (_, load_v, _) in enumerate(blocks):
        p = jnp.exp(s_ref[i] - m)
        p_sum = p if p_sum is None else p_sum + p
        d = _dot(p.astype(BF16), load_v())
        acc = d if acc is None else acc + d
    return m, jnp.sum(p_sum, -1, keepdims=True), acc


def _per_query_tile(qt, n_tiles, body):
    for n in range(1, n_tiles + 1):
        pl.when(qt == n - 1)(functools.partial(body, n))


def _cast_once(qt, pairs):
    @pl.when(qt == 0)
    def _cast():
        for src, dst in pairs:
            dst[...] = src[...].astype(BF16)


def _causal_blocks(n, t, kb_ref, vb_ref, diag_fn, left_fn, far_fn):
    blocks = []
    for kt in range(n):
        fn = diag_fn if kt == n - 1 else (left_fn if kt == n - 2 else far_fn)
        blocks.append((lambda kt=kt: kb_ref[kt * t:(kt + 1) * t, :], lambda kt=kt: vb_ref[kt * t:(kt + 1) * t, :],
                       functools.partial(fn, kt)))
    return blocks


def _diff_prompt_kernel(q_ref, k_ref, v_ref, bias_ref, far_ref, lam_ref, sub_ref, o_ref, kb_ref, vb_ref, s_ref,
                        *, lam_init):
    t = q_ref.shape[0]
    qt = pl.program_id(2)
    _cast_once(qt, ((k_ref, kb_ref), (v_ref, vb_ref)))
    q = q_ref[...]
    lane = lax.broadcasted_iota(jnp.int32, q.shape, 1)
    q1 = jnp.where(lane < DIFF_DK, q, 0.0).astype(BF16)
    q2 = jnp.where(lane >= DIFF_DK, q, 0.0).astype(BF16)
    scale = DIFF_DK ** -0.5
    far = far_ref[...][:, :1]
    lam = lam_ref[...]
    lam_full = (jnp.exp(jnp.sum(lam[0:1] * lam[1:2], keepdims=True))
                - jnp.exp(jnp.sum(lam[2:3] * lam[3:4], keepdims=True)) + lam_init)

    def body(n):
        blocks = _causal_blocks(n, t, kb_ref, vb_ref,
                                lambda kt, s: s * scale + bias_ref[0],
                                lambda kt, s: s * scale + bias_ref[1],
                                lambda kt, s: s * scale + far)
        _, l1, a1 = _attend_blocks(q1, blocks, s_ref.at[0])
        _, l2, a2 = _attend_blocks(q2, blocks, s_ref.at[1])
        o = a1 / l1 - lam_full * (a2 / l2)
        o = o * lax.rsqrt(jnp.mean(o * o, -1, keepdims=True) + RMS_EPS) * sub_ref[...] * (1.0 - lam_init)
        o_ref[...] = o.astype(o_ref.dtype)

    _per_query_tile(qt, k_ref.shape[0] // t, body)


def diff_prompt(proj, tiles, far, lam, subln, lam_init):
    b, s, _ = proj.shape
    t = ATT_TILE
    kern = functools.partial(_diff_prompt_kernel, lam_init=lam_init)
    return pl.pallas_call(
        kern,
        grid=(b, DIFF_HEADS, s // t),
        in_specs=[pl.BlockSpec((None, t, HEAD_DIM), lambda bi, h, qt: (bi, qt, h)),
                  pl.BlockSpec((None, s, HEAD_DIM), lambda bi, h, qt: (bi, 0, DIFF_HEADS + h)),
                  pl.BlockSpec((None, s, HEAD_DIM), lambda bi, h, qt: (bi, 0, 2 * DIFF_HEADS + h)),
                  pl.BlockSpec((None, 2, t, t), lambda bi, h, qt: (h, 0, 0, 0)),
                  pl.BlockSpec((None, 1, LANES), lambda bi, h, qt: (h, 0, 0)),
                  pl.BlockSpec(lam.shape, lambda bi, h, qt: (0, 0)),
                  pl.BlockSpec((1, HEAD_DIM), lambda bi, h, qt: (0, 0))],
        out_specs=pl.BlockSpec((None, t, HEAD_DIM), lambda bi, h, qt: (bi, qt, h)),
        out_shape=jax.ShapeDtypeStruct((b, s, DIFF_HEADS * HEAD_DIM), BF16),
        scratch_shapes=[pltpu.VMEM((s, HEAD_DIM), BF16), pltpu.VMEM((s, HEAD_DIM), BF16),
                        pltpu.VMEM((2, s // t, t, t), F32)],
        compiler_params=_params(3),
        name="diff_prompt",
    )(proj, proj, proj, tiles, far, lam, subln.reshape(1, HEAD_DIM))


def _cmp_ab_kernel(x_ref, pe_ref, w1_ref, o_ref, *, paged):
    half = w1_ref.shape[0] // 2
    hid = w1_ref.shape[1]
    per = PAGE_SIZE // NSA_CMP_STRIDE
    for g in range(NSA_KV_HEADS):
        if paged:
            n_rows = x_ref.shape[0] * per
            xg = jnp.concatenate(
                [x_ref[:, pl.ds(NSA_KV_HEADS * p + g, per, stride=NSA_KV_HEADS * NSA_CMP_STRIDE), :].reshape(n_rows, HEAD_DIM)
                 for p in range(NSA_CMP_STRIDE)], axis=1)
        else:
            xg = jnp.concatenate([x_ref[:, (NSA_KV_HEADS * p + g) * HEAD_DIM:(NSA_KV_HEADS * p + g + 1) * HEAD_DIM]
                                  for p in range(NSA_CMP_STRIDE)], axis=1)
        top = (xg + pe_ref[0:1, :]).astype(BF16)
        bot = (xg + pe_ref[1:2, :]).astype(BF16)
        o_ref[:, 2 * g * hid:(2 * g + 1) * hid] = _dot(top, w1_ref[:half, :])
        o_ref[:, (2 * g + 1) * hid:(2 * g + 2) * hid] = _dot(bot, w1_ref[half:, :])


def _cmp_fin_kernel(*refs, n_prefetch):
    ab_refs, w2_ref, o_ref = refs[n_prefetch:-2], refs[-2], refs[-1]
    ab = jnp.concatenate([r[...] for r in ab_refs], axis=0) if len(ab_refs) > 1 else ab_refs[0][...]
    n = ab.shape[0]
    hid = w2_ref.shape[0]
    row = lax.broadcasted_iota(jnp.int32, (n, HEAD_DIM), 0)
    for g in range(NSA_KV_HEADS):
        a = ab[:, 2 * g * hid:(2 * g + 1) * hid]
        b = ab[:, (2 * g + 1) * hid:(2 * g + 2) * hid]
        hidden = jax.nn.gelu(a + pltpu.roll(b, n - 1, 0), approximate=True)
        out = _dot(hidden.astype(BF16), w2_ref[...])
        o_ref[g] = jnp.where(row < n - 1, out, 0.0)


def nsa_compress(kseq, pe, w1, w2, page_table):
    rows = kseq.shape[0] * kseq.shape[1] // NSA_CMP_STRIDE
    width = NSA_CMP_STRIDE * NSA_KV_HEADS * HEAD_DIM
    hid = w1.shape[1]
    tr = math.gcd(rows, 256)
    per = PAGE_SIZE // NSA_CMP_STRIDE
    if page_table is None:
        x = kseq.reshape(rows, width)
        x_spec = pl.BlockSpec((tr, width), lambda i: (i, 0))
    else:
        x = kseq.reshape(kseq.shape[0], PAGE_SIZE * NSA_KV_HEADS, HEAD_DIM)
        x_spec = pl.BlockSpec((tr // per, PAGE_SIZE * NSA_KV_HEADS, HEAD_DIM), lambda i: (i, 0, 0))
    ab = pl.pallas_call(
        functools.partial(_cmp_ab_kernel, paged=page_table is not None),
        grid=(rows // tr,),
        in_specs=[x_spec,
                  pl.BlockSpec((2, width // 2), lambda i: (0, 0)),
                  pl.BlockSpec(w1.shape, lambda i: (0, 0))],
        out_specs=pl.BlockSpec((tr, 4 * hid), lambda i: (i, 0)),
        out_shape=jax.ShapeDtypeStruct((rows, 4 * hid), F32),
        compiler_params=_params(1),
        name="nsa_cmp_layer1",
    )(x, pe.reshape(2, width // 2), w1)
    n_cmp = 128
    out_block = pl.BlockSpec((None, NSA_KV_HEADS, n_cmp, HEAD_DIM), lambda b, *_: (b, 0, 0, 0))
    if page_table is None:
        nb = kseq.shape[0]
        return pl.pallas_call(
            functools.partial(_cmp_fin_kernel, n_prefetch=0),
            grid=(nb,),
            in_specs=[pl.BlockSpec((n_cmp, 4 * hid), lambda b: (b, 0)), pl.BlockSpec(w2.shape, lambda b: (0, 0))],
            out_specs=out_block,
            out_shape=jax.ShapeDtypeStruct((nb, NSA_KV_HEADS, n_cmp, HEAD_DIM), F32),
            compiler_params=_params(1),
            name="nsa_cmp_layer2",
        )(ab, w2)
    nb, n_pages = page_table.shape
    per = PAGE_SIZE // NSA_CMP_STRIDE
    ab3 = ab.reshape(rows // per, per, 4 * hid)
    grid_spec = pltpu.PrefetchScalarGridSpec(
        num_scalar_prefetch=1,
        grid=(nb,),
        in_specs=[pl.BlockSpec((None, per, 4 * hid), lambda b, pt, p=p: (pt[b, p], 0, 0)) for p in range(n_pages)]
        + [pl.BlockSpec(w2.shape, lambda b, pt: (0, 0))],
        out_specs=out_block,
    )
    return pl.pallas_call(
        functools.partial(_cmp_fin_kernel, n_prefetch=1),
        grid_spec=grid_spec,
        out_shape=jax.ShapeDtypeStruct((nb, NSA_KV_HEADS, n_cmp, HEAD_DIM), F32),
        compiler_params=_params(1),
        name="nsa_cmp_layer2_paged",
    )(page_table, *([ab3] * n_pages), w2)


def _rank_select(score, lane, n_valid_lanes, topk):
    rank = jnp.zeros(score.shape, jnp.int32)
    for j in range(n_valid_lanes):
        col = score[:, j:j + 1]
        beats = (col > score) | ((col == score) & (j < lane))
        rank = rank + beats.astype(jnp.int32)
    return rank < topk


def _nsa_cmp_kernel(q_ref, kc_ref, vc_ref, bias_ref, cov_ref, o_ref, sel_ref, *, qpos0, n_slc):
    t = q_ref.shape[0]
    qt = pl.program_id(2)
    kc = kc_ref[...].astype(BF16)
    vc = vc_ref[...].astype(BF16)
    scale = HEAD_DIM ** -0.5
    imp = jnp.zeros((t, LANES), F32)
    for r in range(NSA_GROUP):
        qr = q_ref[:, r * HEAD_DIM:(r + 1) * HEAD_DIM].astype(BF16)
        bias = bias_ref[r]
        live = bias > 0.5 * NEG
        s = _dot_nt(qr, kc) * scale + bias
        m = jnp.max(s, -1, keepdims=True)
        p = jnp.where(live, jnp.exp(s - m), 0.0)
        p = p / jnp.maximum(jnp.sum(p, -1, keepdims=True), jnp.finfo(F32).tiny)
        pb = p.astype(BF16)
        o_ref[:, r * HEAD_DIM:(r + 1) * HEAD_DIM] = _dot(pb, vc)
        imp = imp + _dot(pb, cov_ref[...])
    lane = lax.broadcasted_iota(jnp.int32, (t, LANES), 1)
    qpos = qpos0 + qt * t + lax.broadcasted_iota(jnp.int32, (t, LANES), 0)
    cur = qpos // NSA_SEL_BLOCK
    valid = (lane * NSA_SEL_BLOCK <= qpos) & (lane < n_slc)
    forced = valid & ((lane == 0) | (lane == cur) | (lane == cur - 1))
    score = jnp.where(forced, jnp.inf, jnp.where(valid, imp, -jnp.inf))
    chosen = _rank_select(score, lane, n_slc, min(NSA_SEL_TOPK, n_slc))
    sel_ref[...] = (valid & chosen).astype(F32)


def nsa_cmp_select(q_arr, q_col0, t, kcmp, vcmp, tab_n, qpos0, seq_len):
    b, sq, _ = q_arr.shape
    n_cmp = (seq_len - NSA_CMP_LEN) // NSA_CMP_STRIDE + 1
    n_slc = -(-seq_len // NSA_SEL_BLOCK)
    i = jnp.arange(LANES)
    qpos = qpos0 + jnp.arange(sq)
    dist = qpos[:, None] - (i * NSA_CMP_STRIDE + NSA_CMP_LEN - 1)[None, :]
    bias = _bias_of(tab_n, dist, (dist >= 0) & (i < n_cmp)[None, :]).reshape(NSA_KV_HEADS, NSA_GROUP, sq, LANES)
    cs = np.arange(LANES)[:, None] * NSA_CMP_STRIDE
    ss = np.arange(LANES)[None, :] * NSA_SEL_BLOCK
    covers = ((cs < ss + NSA_SEL_BLOCK) & (cs + NSA_CMP_LEN > ss)
              & (np.arange(LANES)[:, None] < n_cmp) & (np.arange(LANES)[None, :] < n_slc))
    covers = jnp.asarray(covers, BF16)
    kern = functools.partial(_nsa_cmp_kernel, qpos0=qpos0, n_slc=n_slc)
    gw = NSA_GROUP * HEAD_DIM
    return pl.pallas_call(
        kern,
        grid=(b, NSA_KV_HEADS, sq // t),
        in_specs=[pl.BlockSpec((None, t, gw), lambda bi, g, qt: (bi, qt, q_col0 + g)),
                  pl.BlockSpec((None, None, LANES, HEAD_DIM), lambda bi, g, qt: (bi, g, 0, 0)),
                  pl.BlockSpec((None, None, LANES, HEAD_DIM), lambda bi, g, qt: (bi, g, 0, 0)),
                  pl.BlockSpec((None, NSA_GROUP, t, LANES), lambda bi, g, qt: (g, 0, qt, 0)),
                  pl.BlockSpec((LANES, LANES), lambda bi, g, qt: (0, 0))],
        out_specs=[pl.BlockSpec((None, t, gw), lambda bi, g, qt: (bi, qt, g)),
                   pl.BlockSpec((None, None, t, LANES), lambda bi, g, qt: (bi, g, qt, 0))],
        out_shape=[jax.ShapeDtypeStruct((b, sq, NSA_KV_HEADS * gw), F32),
                   jax.ShapeDtypeStruct((b, NSA_KV_HEADS, sq, LANES), F32)],
        compiler_params=_params(3),
        name="nsa_cmp_select",
    )(q_arr, kcmp, vcmp, bias, covers)


def _gate_pad(g):
    lead = g.shape[:-1]
    g = g.reshape(*lead, NSA_KV_HEADS, NSA_GROUP * 3)
    g = jnp.pad(g, [(0, 0)] * (len(lead) + 1) + [(0, LANES - NSA_GROUP * 3)])
    return g.reshape(*lead, NSA_KV_HEADS * LANES)


def _nsa_prompt_kernel(q_ref, ks_ref, vs_ref, kw_ref, vw_ref, sel_ref, ocmp_ref, g_ref, bias_ref, far_ref, e_ref,
                       o_ref, ksb_ref, vsb_ref, kwb_ref, vwb_ref, s_ref, sw_ref):
    t = q_ref.shape[0]
    qt = pl.program_id(2)
    scale = HEAD_DIM ** -0.5
    _cast_once(qt, ((ks_ref, ksb_ref), (vs_ref, vsb_ref), (kw_ref, kwb_ref), (vw_ref, vwb_ref)))
    q4 = jnp.concatenate([q_ref[:, r * HEAD_DIM:(r + 1) * HEAD_DIM] for r in range(NSA_GROUP)], axis=0).astype(BF16)
    sel4 = jnp.concatenate([sel_ref[...].astype(BF16)] * NSA_GROUP, axis=0)
    far = far_ref[...][:, :1]
    rows = NSA_GROUP * t

    def chosen(kt):
        return _dot(sel4, e_ref[:, kt * t:(kt + 1) * t]) > 0.5

    def body(n):
        slc = _causal_blocks(n, t, ksb_ref, vsb_ref,
                             lambda kt, s: jnp.where(chosen(kt), s * scale + bias_ref[0], NEG),
                             lambda kt, s: jnp.where(chosen(kt), s * scale + bias_ref[1], NEG),
                             lambda kt, s: jnp.where(chosen(kt), s * scale + far, NEG))
        _, l_s, a_s = _attend_blocks(q4, slc, s_ref)

        def edge(kt, s):
            query = lax.broadcasted_iota(jnp.int32, (rows, t), 0) & (t - 1)
            key = lax.broadcasted_iota(jnp.int32, (rows, t), 1)
            return jnp.where(key > query, s * scale + far, NEG)

        win = _causal_blocks(n, t, kwb_ref, vwb_ref,
                             lambda kt, s: s * scale + bias_ref[0],
                             lambda kt, s: s * scale + bias_ref[1],
                             edge)[max(n - 3, 0):]
        _, l_w, a_w = _attend_blocks(q4, win, sw_ref)
        o_slc = a_s / l_s
        o_win = a_w / l_w
        gate = jax.nn.sigmoid(g_ref[...])
        for r in range(NSA_GROUP):
            o = (gate[:, 3 * r:3 * r + 1] * ocmp_ref[:, r * HEAD_DIM:(r + 1) * HEAD_DIM]
                 + gate[:, 3 * r + 1:3 * r + 2] * o_slc[r * t:(r + 1) * t]
                 + gate[:, 3 * r + 2:3 * r + 3] * o_win[r * t:(r + 1) * t])
            o_ref[:, r * HEAD_DIM:(r + 1) * HEAD_DIM] = o.astype(o_ref.dtype)

    _per_query_tile(qt, ks_ref.shape[0] // t, body)


def nsa_prompt(proj, sel, o_cmp, gates, tiles, far):
    b, s, _ = proj.shape
    t = ATT_TILE
    assert 2 * t == NSA_WINDOW
    nkt = s // t
    gw = NSA_GROUP * HEAD_DIM
    kv0 = (2 * DIFF_HEADS * 2 * DIFF_DK + DIFF_HEADS * HEAD_DIM + NSA_HEADS * HEAD_DIM) // HEAD_DIM + 2 * NSA_KV_HEADS
    expand = jnp.asarray(np.arange(LANES)[:, None] == np.arange(s)[None, :] // NSA_SEL_BLOCK, BF16)
    rows = NSA_GROUP * t
    tiles4 = jnp.swapaxes(tiles.reshape(NSA_KV_HEADS, NSA_GROUP, 2, t, t), 1, 2).reshape(NSA_KV_HEADS, 2, rows, t)
    far4 = jnp.broadcast_to(far.reshape(NSA_KV_HEADS, NSA_GROUP, 1, LANES),
                            (NSA_KV_HEADS, NSA_GROUP, t, LANES)).reshape(NSA_KV_HEADS, rows, LANES)
    kv = lambda off: pl.BlockSpec((None, s, HEAD_DIM), lambda bi, g, qt: (bi, 0, kv0 + off + g))
    return pl.pallas_call(
        _nsa_prompt_kernel,
        grid=(b, NSA_KV_HEADS, nkt),
        in_specs=[pl.BlockSpec((None, t, gw), lambda bi, g, qt: (bi, qt, 6 + g)),
                  kv(0), kv(2), kv(4), kv(6),
                  pl.BlockSpec((None, None, t, LANES), lambda bi, g, qt: (bi, g, qt, 0)),
                  pl.BlockSpec((None, t, gw), lambda bi, g, qt: (bi, qt, g)),
                  pl.BlockSpec((None, t, LANES), lambda bi, g, qt: (bi, qt, g)),
                  pl.BlockSpec((None, 2, rows, t), lambda bi, g, qt: (g, 0, 0, 0)),
                  pl.BlockSpec((None, rows, LANES), lambda bi, g, qt: (g, 0, 0)),
                  pl.BlockSpec((LANES, s), lambda bi, g, qt: (0, 0))],
        out_specs=pl.BlockSpec((None, t, gw), lambda bi, g, qt: (bi, qt, g)),
        out_shape=jax.ShapeDtypeStruct((b, s, NSA_KV_HEADS * gw), BF16),
        scratch_shapes=[pltpu.VMEM((s, HEAD_DIM), BF16)] * 4
        + [pltpu.VMEM((nkt, rows, t), F32), pltpu.VMEM((min(nkt, 3), rows, t), F32)],
        compiler_params=_params(3),
        name="nsa_prompt",
    )(proj, proj, proj, proj, proj, sel, o_cmp, gates, tiles4, far4, expand)


def _moba_prompt_kernel(q_ref, k_ref, v_ref, bias_ref, far_ref, o_ref, kmean_ref, kb_ref, vb_ref, s_ref):
    t = q_ref.shape[0]
    qt = pl.program_id(2)
    nb = k_ref.shape[0] // t
    scale = HEAD_DIM ** -0.5
    _cast_once(qt, ((k_ref, kb_ref), (v_ref, vb_ref)))

    @pl.when(qt == 0)
    def _means():
        kmean_ref[...] = jnp.zeros_like(kmean_ref)
        for j in range(nb):
            kmean_ref[j:j + 1, :] = jnp.mean(k_ref[j * t:(j + 1) * t, :], axis=0, keepdims=True)

    q = q_ref[...].astype(BF16)
    lane = lax.broadcasted_iota(jnp.int32, (t, LANES), 1)
    score = jnp.where(lane < qt, _dot_nt(q, kmean_ref[...].astype(BF16)), -jnp.inf)
    sel = (lane < qt) & _rank_select(score, lane, nb, MOBA_TOPK)
    far = far_ref[...][:, :1]

    def body(n):
        blocks = _causal_blocks(n, t, kb_ref, vb_ref,
                                lambda kt, s: s * scale + bias_ref[0],
                                lambda kt, s: jnp.where(sel[:, kt:kt + 1], s * scale + bias_ref[1], NEG),
                                lambda kt, s: jnp.where(sel[:, kt:kt + 1], s * scale + far, NEG))
        _, l, acc = _attend_blocks(q, blocks, s_ref)
        o_ref[...] = (acc / l).astype(o_ref.dtype)

    _per_query_tile(qt, nb, body)


def moba_prompt(proj, tiles, far):
    b, s, _ = proj.shape
    t = ATT_TILE
    assert t == MOBA_BLOCK and s // t <= LANES
    return pl.pallas_call(
        _moba_prompt_kernel,
        grid=(b, MOBA_HEADS, s // t),
        in_specs=[pl.BlockSpec((None, t, HEAD_DIM), lambda bi, h, qt: (bi, qt, h)),
                  pl.BlockSpec((None, s, HEAD_DIM), lambda bi, h, qt: (bi, 0, MOBA_HEADS + h)),
                  pl.BlockSpec((None, s, HEAD_DIM), lambda bi, h, qt: (bi, 0, 2 * MOBA_HEADS + h)),
                  pl.BlockSpec((None, 2, t, t), lambda bi, h, qt: (h, 0, 0, 0)),
                  pl.BlockSpec((None, 1, LANES), lambda bi, h, qt: (h, 0, 0))],
        out_specs=pl.BlockSpec((None, t, HEAD_DIM), lambda bi, h, qt: (bi, qt, h)),
        out_shape=jax.ShapeDtypeStruct((b, s, MOBA_HEADS * HEAD_DIM), BF16),
        scratch_shapes=[pltpu.VMEM((LANES, HEAD_DIM), F32), pltpu.VMEM((s, HEAD_DIM), BF16),
                        pltpu.VMEM((s, HEAD_DIM), BF16), pltpu.VMEM((s // t, t, t), F32)],
        compiler_params=_params(3),
        name="moba_prompt",
    )(proj, proj, proj, tiles, far)


def _decode_bias(tab, row_t, row_col, row_kv, n_kv, qpos0, kpos, window=None):
    c = kpos.shape[0] * n_kv
    col_kv = np.arange(c) % n_kv
    kp = np.repeat(kpos, n_kv)
    dist = (qpos0 + row_t)[:, None] - kp[None, :]
    ok = (row_kv[:, None] == col_kv[None, :]) & (dist >= 0) & (kp >= 0)[None, :]
    if window is not None:
        ok = ok & (dist < window)
    per_row = jnp.stack([tab[:, int(h)] for h in row_col], axis=1).astype(F32)
    b = _lookup(per_row[:, :, None], _t5_bucket(jnp.asarray(dist)))
    return jnp.where(jnp.asarray(ok), b, NEG)


def _decode_kernel(*refs, scale, ppc, n_pages, masked, diff_lam_init):
    pt_ref, q_ref, kn_ref, vn_ref, bnew_ref, tiles_ref = refs[:6]
    pos = 6
    rm_ref = None
    if masked:
        rm_ref = refs[pos]
        pos += 1
    lam_ref = sub_ref = None
    if diff_lam_init is not None:
        lam_ref, sub_ref = refs[pos:pos + 2]
        pos += 2
    k_refs = refs[pos:pos + ppc]
    v_refs = refs[pos + ppc:pos + 2 * ppc]
    o_ref, m_ref, l_ref, acc_ref, s_ref = refs[pos + 2 * ppc:]
    single = ppc == n_pages
    c = 0 if single else pl.program_id(1)
    q = q_ref[...].astype(BF16)
    rows = q.shape[0]
    cols = k_refs[0].shape[0]

    @pl.when(pl.program_id(1) == 0)
    def _init():
        m_ref[...] = jnp.full_like(m_ref, NEG)
        l_ref[...] = jnp.zeros_like(l_ref)
        acc_ref[...] = jnp.zeros_like(acc_ref)

    s_new = _dot_nt(q, kn_ref[...].astype(BF16)) * scale + bnew_ref[...]
    if not single:
        s_new = jnp.where(c == 0, s_new, NEG)

    def page_logits(p, s):
        gp = c * ppc + p
        if single:
            tile = tiles_ref[0 if gp == 0 else (2 if gp == n_pages - 1 else 1)]
        else:
            tile = tiles_ref[1]
            if p == 0:
                tile = jnp.where(gp == 0, tiles_ref[0], tile)
            if p == ppc - 1:
                tile = jnp.where(gp == n_pages - 1, tiles_ref[2], tile)
        s = s * scale + tile
        if masked:
            assert single
            col = lax.broadcasted_iota(jnp.int32, (rows, cols), 1)
            keep = jnp.where(col < cols // 2, rm_ref[:, 2 * gp:2 * gp + 1], rm_ref[:, 2 * gp + 1:2 * gp + 2])
            s = jnp.where(keep > 0.5, s, NEG)
        return s

    blocks = [(lambda p=p: k_refs[p][...].astype(BF16), lambda p=p: v_refs[p][...].astype(BF16),
               functools.partial(page_logits, p)) for p in range(ppc)]
    m_old = m_ref[...]
    m, l_pages, acc_pages = _attend_blocks(q, blocks, s_ref,
                                           m_floor=jnp.maximum(m_old, jnp.max(s_new, -1, keepdims=True)))
    alpha = jnp.exp(m_old - m)
    p_new = jnp.exp(s_new - m)
    l_ref[...] = alpha * l_ref[...] + l_pages + jnp.sum(p_new, -1, keepdims=True)
    acc_ref[...] = alpha * acc_ref[...] + acc_pages + _dot(p_new.astype(BF16), vn_ref[...].astype(BF16))
    m_ref[...] = m

    @pl.when(pl.program_id(1) == pl.num_programs(1) - 1)
    def _finish():
        o = acc_ref[...] / l_ref[...]
        if diff_lam_init is not None:
            lam = lam_ref[...]
            lam_full = (jnp.exp(jnp.sum(lam[0:1] * lam[1:2], keepdims=True))
                        - jnp.exp(jnp.sum(lam[2:3] * lam[3:4], keepdims=True)) + diff_lam_init)
            half = rows // 2
            o = o[:half] - lam_full * o[half:]
            o = o * lax.rsqrt(jnp.mean(o * o, -1, keepdims=True) + RMS_EPS) * sub_ref[...] * (1.0 - diff_lam_init)
        o_ref[...] = o.astype(o_ref.dtype)


def decode_attn(page_table, q, k_new, v_new, bias_new, tiles, k_pages, v_pages, n_chunks, scale, out_dtype,
                row_mask=None, diff=None, name="decode_attn"):
    bs, r, _ = q.shape
    n_pages = page_table.shape[1]
    ppc = n_pages // n_chunks
    c = k_pages.shape[1]
    per_b = lambda shape: pl.BlockSpec((None,) + shape, lambda b, ch, pt: (b, 0, 0))
    const = lambda shape: pl.BlockSpec(shape, lambda b, ch, pt: (0,) * len(shape))
    in_specs = [per_b((r, HEAD_DIM)), per_b((LANES, HEAD_DIM)), per_b((LANES, HEAD_DIM)),
                const((r, LANES)), const((3, r, c))]
    args = [q, k_new, v_new, bias_new, tiles]
    if row_mask is not None:
        in_specs.append(per_b((r, LANES)))
        args.append(row_mask)
    r_out = r
    if diff is not None:
        in_specs += [const(diff[0].shape), const((1, HEAD_DIM))]
        args += [diff[0], diff[1].reshape(1, HEAD_DIM)]
        r_out = r // 2
    page = lambda p: pl.BlockSpec((None, c, HEAD_DIM), lambda b, ch, pt: (pt[b, ch * ppc + p], 0, 0))
    in_specs += [page(p) for p in range(ppc)] * 2
    args += [k_pages] * ppc + [v_pages] * ppc
    kern = functools.partial(_decode_kernel, scale=scale, ppc=ppc, n_pages=n_pages, masked=row_mask is not None,
                             diff_lam_init=None if diff is None else diff[2])
    grid_spec = pltpu.PrefetchScalarGridSpec(
        num_scalar_prefetch=1,
        grid=(bs, n_chunks),
        in_specs=in_specs,
        out_specs=pl.BlockSpec((None, r_out, HEAD_DIM), lambda b, ch, pt: (b, 0, 0)),
        scratch_shapes=[pltpu.VMEM((r, 1), F32), pltpu.VMEM((r, 1), F32), pltpu.VMEM((r, HEAD_DIM), F32),
                        pltpu.VMEM((ppc, r, c), F32)],
    )
    return pl.pallas_call(
        kern,
        grid_spec=grid_spec,
        out_shape=jax.ShapeDtypeStruct((bs, r_out, HEAD_DIM), out_dtype),
        compiler_params=_params(2),
        name=name,
    )(page_table, *args)


def _pad_rows(x, n):
    return jnp.pad(x, ((0, 0), (0, n - x.shape[1]), (0, 0)))


def _moba_decode_kernel(pt_ref, q_ref, kn_ref, vn_ref, bnew_ref, tiles_ref, k0_ref, k1_ref, v0_ref, v1_ref, o_ref,
                        ssel_ref, mb_ref, lb_ref, ab_ref, s_ref):
    j = pl.program_id(1)
    nb = pl.num_programs(1)
    n_blk = mb_ref.shape[0]
    q = q_ref[...].astype(BF16)
    rows = q.shape[0]
    scale = HEAD_DIM ** -0.5
    lane = lax.broadcasted_iota(jnp.int32, (rows, LANES), 1)

    @pl.when(j == 0)
    def _init():
        ssel_ref[...] = jnp.full_like(ssel_ref, -jnp.inf)

    ksum = (jnp.sum(k0_ref[...].reshape(PAGE_SIZE, MOBA_HEADS, HEAD_DIM), axis=0)
            + jnp.sum(k1_ref[...].reshape(PAGE_SIZE, MOBA_HEADS, HEAD_DIM), axis=0))
    blocks = [(lambda: k0_ref[...].astype(BF16), lambda: v0_ref[...].astype(BF16),
               lambda s: s * scale + tiles_ref[0]),
              (lambda: k1_ref[...].astype(BF16), lambda: v1_ref[...].astype(BF16),
               lambda s: s * scale + jnp.where(j == nb - 1, tiles_ref[1], tiles_ref[0]))]
    m_blk, l_blk, acc_blk = _attend_blocks(q, blocks, s_ref)
    mb_ref[j] = m_blk
    lb_ref[j] = l_blk
    ab_ref[j] = acc_blk
    kmean = (ksum * (1.0 / MOBA_BLOCK)).astype(BF16).astype(F32)
    kmean_rows = jnp.concatenate([kmean] * (rows // MOBA_HEADS), axis=0)
    score = jnp.sum(q.astype(F32) * kmean_rows, -1, keepdims=True)
    ssel_ref[...] = jnp.where(lane == j, score, ssel_ref[...])

    @pl.when(j == nb - 1)
    def _finish():
        chosen = (lane < n_blk) & _rank_select(ssel_ref[...], lane, n_blk, MOBA_TOPK)
        sel = chosen.astype(F32)
        s_new = _dot_nt(q, kn_ref[...].astype(BF16)) * scale + bnew_ref[...]
        keep = [jnp.sum(jnp.where(lane == b, sel, 0.0), -1, keepdims=True) > 0.5 for b in range(n_blk)]
        m_tot = jnp.max(s_new, -1, keepdims=True)
        for b in range(n_blk):
            m_tot = jnp.maximum(m_tot, jnp.where(keep[b], mb_ref[b], NEG))
        p_new = jnp.exp(s_new - m_tot)
        l_tot = jnp.sum(p_new, -1, keepdims=True)
        acc = _dot(p_new.astype(BF16), vn_ref[...].astype(BF16))
        for b in range(n_blk):
            w = jnp.where(keep[b], jnp.exp(mb_ref[b] - m_tot), 0.0)
            l_tot = l_tot + w * lb_ref[b]
            acc = acc + w * ab_ref[b]
        o_ref[...] = (acc / l_tot).astype(o_ref.dtype)


def moba_decode(page_table, q, k_new, v_new, bias_new, tiles, k_pages, v_pages):
    bs, r, _ = q.shape
    n_pages = page_table.shape[1]
    ppb = MOBA_BLOCK // PAGE_SIZE
    assert ppb == 2 and n_pages % ppb == 0 and n_pages // ppb >= MOBA_TOPK
    n_blk = n_pages // ppb
    c = k_pages.shape[1]
    per_b = lambda shape: pl.BlockSpec((None,) + shape, lambda b, j, pt: (b, 0, 0))
    const = lambda shape: pl.BlockSpec(shape, lambda b, j, pt: (0,) * len(shape))
    page = lambda p: pl.BlockSpec((None, c, HEAD_DIM), lambda b, j, pt: (pt[b, ppb * j + p], 0, 0))
    grid_spec = pltpu.PrefetchScalarGridSpec(
        num_scalar_prefetch=1,
        grid=(bs, n_blk),
        in_specs=[per_b((r, HEAD_DIM)), per_b((LANES, HEAD_DIM)), per_b((LANES, HEAD_DIM)),
                  const((r, LANES)), const((2, r, c)), page(0), page(1), page(0), page(1)],
        out_specs=pl.BlockSpec((None, r, HEAD_DIM), lambda b, j, pt: (b, 0, 0)),
        scratch_shapes=[pltpu.VMEM((r, LANES), F32), pltpu.VMEM((n_blk, r, 1), F32), pltpu.VMEM((n_blk, r, 1), F32),
                        pltpu.VMEM((n_blk, r, HEAD_DIM), F32), pltpu.VMEM((ppb, r, c), F32)],
    )
    return pl.pallas_call(
        _moba_decode_kernel,
        grid_spec=grid_spec,
        out_shape=jax.ShapeDtypeStruct((bs, r, HEAD_DIM), BF16),
        compiler_params=_params(2),
        name="moba_decode",
    )(page_table, q, k_new, v_new, bias_new, tiles, k_pages, k_pages, v_pages, v_pages)


def _nsa_combine_kernel(oc_ref, os_ref, ow_ref, g_ref, o_ref):
    gate = jax.nn.sigmoid(g_ref[...])
    o = gate[..., 0:1] * oc_ref[...] + gate[..., 1:2] * os_ref[...] + gate[..., 2:3] * ow_ref[...]
    o_ref[...] = o.astype(o_ref.dtype)


def nsa_combine(o_cmp, o_slc, o_win, gates):
    full = lambda a: pl.BlockSpec(a.shape, lambda: (0,) * a.ndim)
    return pl.pallas_call(
        _nsa_combine_kernel,
        in_specs=[full(o_cmp), full(o_slc), full(o_win), full(gates)],
        out_specs=full(o_cmp),
        out_shape=jax.ShapeDtypeStruct(o_cmp.shape, BF16),
        compiler_params=pltpu.CompilerParams(vmem_limit_bytes=VMEM_LIMIT),
        name="nsa_combine",
    )(o_cmp, o_slc, o_win, gates)


AB_MAIN = 2 * DIFF_HEADS * 2 * DIFF_DK + DIFF_HEADS * HEAD_DIM + NSA_HEADS * HEAD_DIM + 6 * NSA_KV_HEADS * HEAD_DIM
ROW_TILE = 512


def _in_proj(h, w, tn):
    m = h.shape[0]
    n = w.shape[1]
    n_main = n // tn * tn
    tm = min(m, ROW_TILE)
    main = matmul(h, w.astype(BF16), n_main, tm, tn, "in_proj")
    if n_main == n:
        return main, None
    tail = jnp.pad(w[:, n_main:], ((0, 0), (0, LANES - (n - n_main)))).astype(BF16)
    return main, matmul(h, tail, LANES, tm, LANES, "in_proj_tail")[:, :n - n_main]


def _even_prompt(proj, gates, b, s, lam, subln, lam_init, pe, w1, w2, tab, tiles, far):
    p3 = proj.reshape(b, s, AB_MAIN)
    o_diff = diff_prompt(p3, tiles[:DIFF_HEADS], far[:DIFF_HEADS], lam, subln, lam_init)
    kv = NSA_KV_HEADS * HEAD_DIM
    c0 = AB_MAIN - 6 * kv
    kc, vc, ks, vs, kw, vw = [p3[..., c0 + i * kv:c0 + (i + 1) * kv] for i in range(6)]
    tab_n = tab[:, DIFF_HEADS:]
    kcmp = nsa_compress(kc, pe[0], w1[0], w2[0], None)
    vcmp = nsa_compress(vc, pe[1], w1[1], w2[1], None)
    o_cmp, sel = nsa_cmp_select(p3, 6, ATT_TILE, kcmp, vcmp, tab_n, 0, s)
    o_nsa = nsa_prompt(p3, sel, o_cmp, _gate_pad(gates).reshape(b, s, NSA_KV_HEADS * LANES),
                       tiles[DIFF_HEADS:], far[DIFF_HEADS:])
    hd = DIFF_HEADS * HEAD_DIM
    keep = min(NSA_WINDOW, s)
    state = (p3[..., hd:2 * hd].reshape(b, s, DIFF_HEADS, HEAD_DIM),
             p3[..., 2 * hd:3 * hd].reshape(b, s, DIFF_HEADS, HEAD_DIM),
             *[a.reshape(b, s, NSA_KV_HEADS, HEAD_DIM) for a in (kc, vc, ks, vs)],
             kw[:, s - keep:].reshape(b, keep, NSA_KV_HEADS, HEAD_DIM),
             vw[:, s - keep:].reshape(b, keep, NSA_KV_HEADS, HEAD_DIM))
    return o_diff.reshape(b * s, hd), o_nsa.reshape(b * s, NSA_HEADS * HEAD_DIM), state


def _even_sample(proj, gates, bs, ts, page_table, cache_dk, cache_dv, cache_ck, cache_cv, cache_sk, cache_sv,
                 win_k, win_v, lam, subln, lam_init, pe, w1, w2, tab):
    n_pool = cache_dk.shape[0]
    n_pages = page_table.shape[1]
    past = n_pages * PAGE_SIZE
    hd = DIFF_HEADS * HEAD_DIM
    kv = NSA_KV_HEADS * HEAD_DIM
    p3 = proj.reshape(bs, ts, AB_MAIN)
    dq, dk, dv, nq = [p3[..., i * hd:(i + 1) * hd] for i in range(4)]
    c0 = AB_MAIN - 6 * kv
    kc, vc, ks, vs, kw, vw = [p3[..., c0 + i * kv:c0 + (i + 1) * kv] for i in range(6)]
    tab_d, tab_n = tab[:, :DIFF_HEADS], tab[:, DIFF_HEADS:]
    new_pos = np.where(np.arange(LANES) < ts, past + np.arange(LANES), -1)

    r = ts * DIFF_HEADS
    q = dq.reshape(bs, r, HEAD_DIM)
    lane = jnp.arange(HEAD_DIM)
    q2 = jnp.concatenate([jnp.where(lane < DIFF_DK, q, 0.0), jnp.where(lane >= DIFF_DK, q, 0.0)], axis=1)
    row_t = np.tile(np.repeat(np.arange(ts), DIFF_HEADS), 2)
    row_h = np.tile(np.arange(DIFF_HEADS), 2 * ts)
    page_pos = lambda p: p * PAGE_SIZE + np.arange(PAGE_SIZE)
    d_bias = lambda kpos: _decode_bias(tab_d, row_t, row_h, row_h, DIFF_HEADS, past, kpos)
    tiles_d = jnp.stack([d_bias(page_pos(p)) for p in (0, 1, n_pages - 1)])
    o_diff = decode_attn(page_table, q2, _pad_rows(dk.reshape(bs, r, HEAD_DIM), LANES),
                         _pad_rows(dv.reshape(bs, r, HEAD_DIM), LANES), d_bias(new_pos[:LANES // DIFF_HEADS]), tiles_d,
                         cache_dk.reshape(n_pool, PAGE_SIZE * DIFF_HEADS, HEAD_DIM),
                         cache_dv.reshape(n_pool, PAGE_SIZE * DIFF_HEADS, HEAD_DIM),
                         2, DIFF_DK ** -0.5, BF16, diff=(lam, subln, lam_init), name="diff_decode")

    rn = ts * NSA_HEADS
    qn = nq.reshape(bs, rn, HEAD_DIM)
    row_tn = np.repeat(np.arange(ts), NSA_HEADS)
    row_cn = np.tile(np.arange(NSA_HEADS), ts)
    row_gn = row_cn // NSA_GROUP
    kcmp = nsa_compress(cache_ck, pe[0], w1[0], w2[0], page_table)
    vcmp = nsa_compress(cache_cv, pe[1], w1[1], w2[1], page_table)
    o_cmp, sel = nsa_cmp_select(p3, 6, ts, kcmp, vcmp, tab_n, past, past + ts)
    row_mask = jnp.broadcast_to(jnp.swapaxes(sel, 1, 2)[:, :, :, None, :], (bs, ts, NSA_KV_HEADS, NSA_GROUP, LANES))
    n_bias = lambda kpos, window=None: _decode_bias(tab_n, row_tn, row_cn, row_gn, NSA_KV_HEADS, past, kpos, window)
    new_n = new_pos[:LANES // NSA_KV_HEADS]
    k_new = lambda a: _pad_rows(a.reshape(bs, ts * NSA_KV_HEADS, HEAD_DIM), LANES)
    tiles_s = jnp.stack([n_bias(page_pos(p)) for p in (0, 1, n_pages - 1)])
    o_slc = decode_attn(page_table, qn, k_new(ks), k_new(vs), n_bias(new_n), tiles_s,
                        cache_sk.reshape(n_pool, PAGE_SIZE * NSA_KV_HEADS, HEAD_DIM),
                        cache_sv.reshape(n_pool, PAGE_SIZE * NSA_KV_HEADS, HEAD_DIM),
                        1, HEAD_DIM ** -0.5, F32, row_mask=row_mask.reshape(bs, rn, LANES), name="nsa_slc_decode")
    w_buf = win_k.shape[1]
    w_pages = w_buf // PAGE_SIZE
    win_table = jnp.arange(bs * w_pages, dtype=jnp.int32).reshape(bs, w_pages)
    win_pos = lambda p: past - w_buf + p * PAGE_SIZE + np.arange(PAGE_SIZE)
    tiles_w = jnp.stack([n_bias(win_pos(p), NSA_WINDOW) for p in (0, 1, w_pages - 1)])
    o_win = decode_attn(win_table, qn, k_new(kw), k_new(vw), n_bias(new_n, NSA_WINDOW), tiles_w,
                        win_k.reshape(bs * w_pages, PAGE_SIZE * NSA_KV_HEADS, HEAD_DIM),
                        win_v.reshape(bs * w_pages, PAGE_SIZE * NSA_KV_HEADS, HEAD_DIM),
                        1, HEAD_DIM ** -0.5, F32, name="nsa_win_decode")
    o_nsa = nsa_combine(o_cmp.reshape(bs, rn, HEAD_DIM), o_slc, o_win, gates.reshape(bs, rn, 3))

    as_kv = lambda a: a.reshape(bs, ts, NSA_KV_HEADS, HEAD_DIM)
    keep = min(NSA_WINDOW, past + ts)
    state = (dk.reshape(bs, ts, DIFF_HEADS, HEAD_DIM), dv.reshape(bs, ts, DIFF_HEADS, HEAD_DIM),
             as_kv(kc), as_kv(vc), as_kv(ks), as_kv(vs),
             jnp.concatenate([win_k, as_kv(kw)], axis=1)[:, w_buf + ts - keep:],
             jnp.concatenate([win_v, as_kv(vw)], axis=1)[:, w_buf + ts - keep:])
    return o_diff.reshape(bs * ts, hd), o_nsa.reshape(bs * ts, NSA_HEADS * HEAD_DIM), state


def _odd_sample(proj, bs, ts, page_table, cache_k, cache_v, tab):
    n_pool = cache_k.shape[0]
    n_pages = page_table.shape[1]
    past = n_pages * PAGE_SIZE
    hd = MOBA_HEADS * HEAD_DIM
    p3 = proj.reshape(bs, ts, 3 * hd)
    q, k, v = [p3[..., i * hd:(i + 1) * hd] for i in range(3)]
    r = ts * MOBA_HEADS
    row_t = np.repeat(np.arange(ts), MOBA_HEADS)
    row_h = np.tile(np.arange(MOBA_HEADS), ts)
    bias = lambda kpos: _decode_bias(tab, row_t, row_h, row_h, MOBA_HEADS, past, kpos)
    page_pos = lambda p: p * PAGE_SIZE + np.arange(PAGE_SIZE)
    new_pos = np.where(np.arange(LANES // MOBA_HEADS) < ts, past + np.arange(LANES // MOBA_HEADS), -1)
    tiles = jnp.stack([bias(page_pos(1)), bias(page_pos(n_pages - 1))])
    o = moba_decode(page_table, q.reshape(bs, r, HEAD_DIM), _pad_rows(k.reshape(bs, r, HEAD_DIM), LANES),
                    _pad_rows(v.reshape(bs, r, HEAD_DIM), LANES), bias(new_pos), tiles,
                    cache_k.reshape(n_pool, PAGE_SIZE * MOBA_HEADS, HEAD_DIM),
                    cache_v.reshape(n_pool, PAGE_SIZE * MOBA_HEADS, HEAD_DIM))
    state = (k.reshape(bs, ts, MOBA_HEADS, HEAD_DIM), v.reshape(bs, ts, MOBA_HEADS, HEAD_DIM))
    return o.reshape(bs * ts, hd), state


def kernel(x_prompt, x_sample, c_prompt, c_sample, page_table, cache_diff_k, cache_diff_v, cache_nsa_cmp_k, cache_nsa_cmp_v, cache_nsa_slc_k, cache_nsa_slc_v, state_nsa_win_k, state_nsa_win_v, cache_moba_k, cache_moba_v, w_in_ab, w_out_ab, diff_lambda, diff_subln_g, nsa_cmp_pe, nsa_cmp_w1, nsa_cmp_w2, w_in_c, w_out_c, rel_bias, ada_w, ada_b, ln_g, ln_b, router_group_w, router_group_b, router_expert_w, router_expert_b, moe_w_gate, moe_w_up, moe_w_down):
    b, s, d = x_prompt.shape
    bs, ts, _ = x_sample.shape
    mp, ms = b * s, bs * ts
    depth = ada_w.shape[0]
    mod = ada_mod(jnp.concatenate([c_prompt, c_sample]), ada_w.reshape(2 * depth, d, 3 * d), ada_b.reshape(2 * depth, 3 * d))

    def mod_vectors(sub):
        per_tok = jnp.repeat(mod[sub, b:], ts, axis=0)
        return ([mod[sub, :b, None, i * d:(i + 1) * d] for i in range(3)],
                [per_tok[None, :, i * d:(i + 1) * d] for i in range(3)])

    xp, xs = x_prompt.reshape(mp, d), x_sample.reshape(ms, d)
    tiles, far = _toeplitz_tiles(rel_bias, ATT_TILE), _far_bias(rel_bias)
    even_p, even_s, odd_p, odd_s = [], [], [], []
    for l in range(depth):
        i = l // 2
        (shp, scp, gp), (shs, scs, gs) = mod_vectors(2 * l)
        hp = mod_cast(xp, shp, scp, ROW_TILE)
        hs = mod_cast(xs, shs, scs, ms)
        if l % 2 == 0:
            lam_init = 0.8 - 0.6 * math.exp(-0.3 * l)
            w1, w2 = nsa_cmp_w1[i].astype(BF16), nsa_cmp_w2[i].astype(BF16)
            proj_p, gates_p = _in_proj(hp, w_in_ab[i], 512)
            proj_s, gates_s = _in_proj(hs, w_in_ab[i], 512)
            oa_p, ob_p, st_p = _even_prompt(proj_p, gates_p, b, s, diff_lambda[i], diff_subln_g[i], lam_init,
                                            nsa_cmp_pe[i], w1, w2, rel_bias, tiles, far)
            oa_s, ob_s, st_s = _even_sample(proj_s, gates_s, bs, ts, page_table, cache_diff_k[i], cache_diff_v[i],
                                            cache_nsa_cmp_k[i], cache_nsa_cmp_v[i], cache_nsa_slc_k[i],
                                            cache_nsa_slc_v[i], state_nsa_win_k[i], state_nsa_win_v[i],
                                            diff_lambda[i], diff_subln_g[i], lam_init, nsa_cmp_pe[i], w1, w2, rel_bias)
            even_p.append(st_p)
            even_s.append(st_s)
            w_out = w_out_ab[i].astype(BF16)
            cols = (0, 0)
        else:
            proj_p, _ = _in_proj(hp, w_in_c[i], 1024)
            proj_s, _ = _in_proj(hs, w_in_c[i], 1024)
            hd = MOBA_HEADS * HEAD_DIM
            p3 = proj_p.reshape(b, s, 3 * hd)
            oa_p = ob_p = moba_prompt(p3, tiles, far).reshape(mp, hd)
            odd_p.append((p3[..., hd:2 * hd].reshape(b, s, MOBA_HEADS, HEAD_DIM),
                          p3[..., 2 * hd:].reshape(b, s, MOBA_HEADS, HEAD_DIM)))
            oa_s, st_s = _odd_sample(proj_s, bs, ts, page_table, cache_moba_k[i], cache_moba_v[i], rel_bias)
            ob_s = oa_s
            odd_s.append(st_s)
            w_out = w_out_c[i].astype(BF16)
            cols = (0, 1)
        xp = out_proj_norm(oa_p, ob_p, cols[0], cols[1], w_out, xp, gp, ln_g[l, 0], ln_b[l, 0], 256)
        xs = out_proj_norm(oa_s, ob_s, cols[0], cols[1], w_out, xs, gs, ln_g[l, 0], ln_b[l, 0], ms)

        (shp, scp, gp), (shs, scs, gs) = mod_vectors(2 * l + 1)
        hp = mod_cast(xp, shp, scp, ROW_TILE)
        hs = mod_cast(xs, shs, scs, ms)
        rw, rb = _router_pack(router_group_w[l], router_group_b[l], router_expert_w[l], router_expert_b[l])
        wg, wu, wd = moe_w_gate[l].astype(BF16), moe_w_up[l].astype(BF16), moe_w_down[l].astype(BF16)
        xp = moe_norm(hp, rw, rb, wg, wu, wd, xp, gp, ln_g[l, 1], ln_b[l, 1], ROW_TILE)
        xs = moe_norm(hs, rw, rb, wg, wu, wd, xs, gs, ln_g[l, 1], ln_b[l, 1], ms)

    stack = lambda rows, j: jnp.stack([r[j] for r in rows])
    outs = [xp.reshape(b, s, d), xs.reshape(bs, ts, d)]
    for j in range(8):
        outs += [stack(even_p, j), stack(even_s, j)]
    for j in range(2):
        outs += [stack(odd_p, j), stack(odd_s, j)]
    return tuple(outs)
```

```python
import functools
import math

import numpy as np
import jax
import jax.numpy as jnp
from jax import lax
from jax.experimental import pallas as pl
from jax.experimental.pallas import tpu as pltpu

F32 = jnp.float32
BF16 = jnp.bfloat16
NEG = -1e30

HEAD_DIM = 128
DIFF_HEADS = 8
DIFF_DK = 64
NSA_HEADS = 8
NSA_KV_HEADS = 2
NSA_GROUP = 4
NSA_CMP_LEN = 32
NSA_CMP_STRIDE = 16
NSA_CMP_HIDDEN = 256
NSA_SEL_BLOCK = 64
NSA_SEL_TOPK = 16
NSA_WINDOW = 512
MOBA_HEADS = 16
MOBA_BLOCK = 256
MOBA_TOPK = 3
PAGE_SIZE = 128
REL_BUCKETS = 32
REL_MAX_DIST = 128
N_GROUPS = 4
EXPERTS_PER_GROUP = 4
N_EXPERTS = 16
DEPTH = 2
DEEPNORM_ALPHA = (2 * DEPTH) ** 0.25
LN_EPS = 1e-5
RMS_EPS = 1e-6

LANES = 128
ATT_TILE = 256
VMEM_LIMIT = 56 * 1024 * 1024


def _params(n_axes, vmem=VMEM_LIMIT):
    return pltpu.CompilerParams(dimension_semantics=("arbitrary",) * n_axes, vmem_limit_bytes=vmem)


def _dot(a, b):
    return jnp.dot(a, b, preferred_element_type=F32)


def _dot_nt(a, b):
    return lax.dot_general(a, b, (((1,), (1,)), ((), ())), preferred_element_type=F32)


def _ada_kernel(c_ref, w_ref, b_ref, o_ref):
    c = c_ref[...]
    a = (c * jax.nn.sigmoid(c)).astype(BF16)
    o_ref[...] = _dot(a, w_ref[...].astype(BF16)) + b_ref[...]


def ada_mod(c_all, ada_w, ada_b):
    n_sub, d, d3 = ada_w.shape
    bc = c_all.shape[0]
    tn = 768
    return pl.pallas_call(
        _ada_kernel,
        grid=(n_sub, d3 // tn),
        in_specs=[pl.BlockSpec((bc, d), lambda s, j: (0, 0)),
                  pl.BlockSpec((None, d, tn), lambda s, j: (s, 0, j)),
                  pl.BlockSpec((None, 1, tn), lambda s, j: (s, 0, j))],
        out_specs=pl.BlockSpec((None, bc, tn), lambda s, j: (s, 0, j)),
        out_shape=jax.ShapeDtypeStruct((n_sub, bc, d3), F32),
        compiler_params=_params(2),
        name="ada_mod",
    )(c_all, ada_w, ada_b.reshape(n_sub, 1, d3))


def _seg_spec(vec, m, tm):
    g, tg, d = vec.shape
    if tg == 1:
        per = m // g // tm
        return pl.BlockSpec((None, 1, d), lambda i, *_: (i // per, 0, 0))
    return pl.BlockSpec((None, tm, d), lambda i, *_: (0, i, 0))


def _mod_kernel(x_ref, sh_ref, sc_ref, o_ref):
    o_ref[...] = (x_ref[...] * (1 + sc_ref[...]) + sh_ref[...]).astype(BF16)


def mod_cast(x, shift, scale, tm):
    m, d = x.shape
    return pl.pallas_call(
        _mod_kernel,
        grid=(m // tm,),
        in_specs=[pl.BlockSpec((tm, d), lambda i: (i, 0)), _seg_spec(shift, m, tm), _seg_spec(scale, m, tm)],
        out_specs=pl.BlockSpec((tm, d), lambda i: (i, 0)),
        out_shape=jax.ShapeDtypeStruct((m, d), BF16),
        compiler_params=_params(1),
        name="mod_cast",
    )(x, shift, scale)


def _mm_kernel(a_ref, w_ref, o_ref, wb_ref):
    @pl.when(pl.program_id(1) == 0)
    def _round_weights():
        wb_ref[...] = w_ref[...].astype(BF16)

    o_ref[...] = _dot(a_ref[...], wb_ref[...]).astype(o_ref.dtype)


def matmul(a, w, n_cols, tm, tn, name):
    m, k = a.shape
    return pl.pallas_call(
        _mm_kernel,
        grid=(n_cols // tn, m // tm),
        in_specs=[pl.BlockSpec((tm, k), lambda j, i: (i, 0)), pl.BlockSpec((k, tn), lambda j, i: (0, j))],
        out_specs=pl.BlockSpec((tm, tn), lambda j, i: (i, j)),
        out_shape=jax.ShapeDtypeStruct((m, n_cols), F32),
        scratch_shapes=[pltpu.VMEM((k, tn), BF16)],
        compiler_params=_params(2),
        name=name,
    )(a, w)


def _cast_kernel(x_ref, o_ref):
    o_ref[...] = x_ref[...].astype(o_ref.dtype)


def cast_bf16(w):
    e, r, c = w.shape
    return pl.pallas_call(
        _cast_kernel,
        grid=(e,),
        in_specs=[pl.BlockSpec((None, r, c), lambda i: (i, 0, 0))],
        out_specs=pl.BlockSpec((None, r, c), lambda i: (i, 0, 0)),
        out_shape=jax.ShapeDtypeStruct(w.shape, BF16),
        compiler_params=_params(1),
        name="cast_bf16",
    )(w)


def _postnorm(x, y, gate, g, b):
    z = DEEPNORM_ALPHA * x + (1 + gate) * y
    zc = z - jnp.mean(z, -1, keepdims=True)
    var = jnp.mean(zc * zc, -1, keepdims=True)
    return zc * lax.rsqrt(var + LN_EPS) * g + b


def _outln_kernel(oa_ref, ob_ref, w_ref, x_ref, gate_ref, g_ref, b_ref, out_ref):
    half = oa_ref.shape[1]
    y = _dot(oa_ref[...], w_ref[:half, :]) + _dot(ob_ref[...], w_ref[half:, :])
    out_ref[...] = _postnorm(x_ref[...], y, gate_ref[...], g_ref[...], b_ref[...])


def out_proj_norm(oa, ob, cols_a, cols_b, w_out, x, gate, ln_g, ln_b, tm):
    m, d = x.shape
    half = w_out.shape[0] // 2
    return pl.pallas_call(
        _outln_kernel,
        grid=(m // tm,),
        in_specs=[pl.BlockSpec((tm, half), lambda i: (i, cols_a)),
                  pl.BlockSpec((tm, half), lambda i: (i, cols_b)),
                  pl.BlockSpec(w_out.shape, lambda i: (0, 0)),
                  pl.BlockSpec((tm, d), lambda i: (i, 0)),
                  _seg_spec(gate, m, tm),
                  pl.BlockSpec((1, d), lambda i: (0, 0)),
                  pl.BlockSpec((1, d), lambda i: (0, 0))],
        out_specs=pl.BlockSpec((tm, d), lambda i: (i, 0)),
        out_shape=jax.ShapeDtypeStruct((m, d), F32),
        compiler_params=_params(1),
        name="out_proj_norm",
    )(oa, ob, w_out, x, gate, ln_g.reshape(1, d), ln_b.reshape(1, d))


def _first_argmax(v, lane, valid):
    vm = jnp.where(valid, v, -jnp.inf)
    top = jnp.max(vm, -1, keepdims=True)
    idx = jnp.min(jnp.where(valid & (vm == top), lane, 1 << 20), -1, keepdims=True)
    return top, idx


def _moe_kernel(h_ref, rw_ref, rb_ref, wg_ref, wu_ref, wd_ref, x_ref, gate_ref, g_ref, b_ref, out_ref,
                comb_ref, acc_ref):
    e = pl.program_id(1)
    h = h_ref[...]

    @pl.when(e == 0)
    def _route():
        logits = _dot(h, rw_ref[...]) + rb_ref[...]
        lane = lax.broadcasted_iota(jnp.int32, logits.shape, 1)
        is_g = lane < N_GROUPS
        gmax = jnp.max(jnp.where(is_g, logits, -jnp.inf), -1, keepdims=True)
        gex = jnp.where(is_g, jnp.exp(logits - gmax), 0.0)
        pg = gex / jnp.sum(gex, -1, keepdims=True)
        pg_top, g_top = _first_argmax(pg, lane, is_g)
        ex_id = lane - N_GROUPS
        in_grp = (ex_id >= g_top * EXPERTS_PER_GROUP) & (ex_id < (g_top + 1) * EXPERTS_PER_GROUP)
        emax = jnp.max(jnp.where(in_grp, logits, -jnp.inf), -1, keepdims=True)
        eex = jnp.where(in_grp, jnp.exp(logits - emax), 0.0)
        pe = eex / jnp.sum(eex, -1, keepdims=True)
        p1, i1 = _first_argmax(pe, lane, in_grp)
        p2, i2 = _first_argmax(pe, lane, in_grp & (lane != i1))
        tot = p1 + p2
        comb = jnp.where(lane == i1, p1 / tot * pg_top, 0.0) + jnp.where(lane == i2, p2 / tot * pg_top, 0.0)
        comb_ref[...] = comb
        acc_ref[...] = jnp.zeros_like(acc_ref)

    lane = lax.broadcasted_iota(jnp.int32, comb_ref.shape, 1)
    c_e = jnp.sum(jnp.where(lane == e + N_GROUPS, comb_ref[...], 0.0), -1, keepdims=True)
    a = _dot(h, wg_ref[...])
    u = _dot(h, wu_ref[...])
    hid = (a * jax.nn.sigmoid(a)) * u * c_e
    acc_ref[...] += _dot(hid.astype(BF16), wd_ref[...])

    @pl.when(e == pl.num_programs(1) - 1)
    def _finish():
        out_ref[...] = _postnorm(x_ref[...], acc_ref[...], gate_ref[...], g_ref[...], b_ref[...])


def _router_pack(wg_r, bg_r, we_r, be_r):
    d = wg_r.shape[0]
    pad = LANES - N_GROUPS - N_EXPERTS
    rw = jnp.concatenate([wg_r, we_r, jnp.zeros((d, pad), F32)], axis=1).astype(BF16)
    rb = jnp.concatenate([bg_r, be_r, jnp.zeros((pad,), F32)]).reshape(1, LANES)
    return rw, rb


def moe_norm(h, rw, rb, wg, wu, wd, x, gate, ln_g, ln_b, tm):
    m, d = x.shape
    n_e, _, f = wg.shape
    return pl.pallas_call(
        _moe_kernel,
        grid=(m // tm, n_e),
        in_specs=[pl.BlockSpec((tm, d), lambda i, e: (i, 0)),
                  pl.BlockSpec(rw.shape, lambda i, e: (0, 0)),
                  pl.BlockSpec(rb.shape, lambda i, e: (0, 0)),
                  pl.BlockSpec((None, d, f), lambda i, e: (e, 0, 0)),
                  pl.BlockSpec((None, d, f), lambda i, e: (e, 0, 0)),
                  pl.BlockSpec((None, f, d), lambda i, e: (e, 0, 0)),
                  pl.BlockSpec((tm, d), lambda i, e: (i, 0)),
                  _seg_spec(gate, m, tm),
                  pl.BlockSpec((1, d), lambda i, e: (0, 0)),
                  pl.BlockSpec((1, d), lambda i, e: (0, 0))],
        out_specs=pl.BlockSpec((tm, d), lambda i, e: (i, 0)),
        out_shape=jax.ShapeDtypeStruct((m, d), F32),
        scratch_shapes=[pltpu.VMEM((tm, LANES), F32), pltpu.VMEM((tm, d), F32)],
        compiler_params=_params(2),
        name="moe_norm",
    )(h, rw, rb, wg, wu, wd, x, gate, ln_g.reshape(1, d), ln_b.reshape(1, d))


def _t5_bucket(dist):
    n = jnp.maximum(dist, 0)
    max_exact = REL_BUCKETS // 2
    nf = jnp.maximum(n, 1).astype(F32)
    log_b = max_exact + (jnp.log(nf / max_exact) / math.log(REL_MAX_DIST / max_exact) * (REL_BUCKETS - max_exact)).astype(jnp.int32)
    return jnp.where(n < max_exact, n, jnp.minimum(log_b, REL_BUCKETS - 1))


def _lookup(cols, bucket):
    out = jnp.zeros(jnp.broadcast_shapes(cols.shape[1:], bucket.shape), F32)
    for k in range(REL_BUCKETS):
        out = jnp.where(bucket == k, cols[k], out)
    return out


def _bias_of(tab, dist, mask):
    cols = tab.astype(F32).reshape(tab.shape + (1,) * dist.ndim)
    return jnp.where(mask, _lookup(cols, _t5_bucket(dist)), NEG)


def _toeplitz_tiles(tab, t):
    i = jnp.arange(t)[:, None]
    j = jnp.arange(t)[None, :]
    diag = _bias_of(tab, i - j, i >= j)
    left = _bias_of(tab, t + i - j, jnp.ones((t, t), bool))
    return jnp.stack([diag, left], axis=1)


def _far_bias(tab):
    return jnp.broadcast_to(tab[REL_BUCKETS - 1].astype(F32)[:, None, None], (tab.shape[1], 1, LANES))


def _attend_blocks(q, blocks, s_ref, m_floor=None):
    m_el = None
    for i, (load_k, _, logits_fn) in enumerate(blocks):
        s = logits_fn(_dot_nt(q, load_k()))
        s_ref[i] = s
        m_el = s if m_el is None else jnp.maximum(m_el, s)
    m = jnp.max(m_el, -1, keepdims=True)
    if m_floor is not None:
        m = jnp.maximum(m, m_floor)
    p_sum = acc = None
    for i, (_, load_v, _) in enumerate(blocks):
        p = jnp.exp(s_ref[i] - m)
        p_sum = p if p_sum is None else p_sum + p
        d = _dot(p.astype(BF16), load_v())
        acc = d if acc is None else acc + d
    return m, jnp.sum(p_sum, -1, keepdims=True), acc


def _per_query_tile(qt, n_tiles, body):
    for n in range(1, n_tiles + 1):
        pl.when(qt == n - 1)(functools.partial(body, n))


def _cast_once(qt, pairs):
    @pl.when(qt == 0)
    def _cast():
        for src, dst in pairs:
            dst[...] = src[...].astype(BF16)


def _causal_blocks(n, t, kb_ref, vb_ref, diag_fn, left_fn, far_fn):
    blocks = []
    for kt in range(n):
        fn = diag_fn if kt == n - 1 else (left_fn if kt == n - 2 else far_fn)
        blocks.append((lambda kt=kt: kb_ref[kt * t:(kt + 1) * t, :], lambda kt=kt: vb_ref[kt * t:(kt + 1) * t, :],
                       functools.partial(fn, kt)))
    return blocks


def _diff_prompt_kernel(q_ref, k_ref, v_ref, bias_ref, far_ref, lam_ref, sub_ref, o_ref, kb_ref, vb_ref, s_ref,
                        *, lam_init):
    t = q_ref.shape[0]
    qt = pl.program_id(2)
    _cast_once(qt, ((k_ref, kb_ref), (v_ref, vb_ref)))
    q = q_ref[...]
    lane = lax.broadcasted_iota(jnp.int32, q.shape, 1)
    q1 = jnp.where(lane < DIFF_DK, q, 0.0).astype(BF16)
    q2 = jnp.where(lane >= DIFF_DK, q, 0.0).astype(BF16)
    scale = DIFF_DK ** -0.5
    far = far_ref[...][:, :1]
    lam = lam_ref[...]
    lam_full = (jnp.exp(jnp.sum(lam[0:1] * lam[1:2], keepdims=True))
                - jnp.exp(jnp.sum(lam[2:3] * lam[3:4], keepdims=True)) + lam_init)

    def body(n):
        blocks = _causal_blocks(n, t, kb_ref, vb_ref,
                                lambda kt, s: s * scale + bias_ref[0],
                                lambda kt, s: s * scale + bias_ref[1],
                                lambda kt, s: s * scale + far)
        _, l1, a1 = _attend_blocks(q1, blocks, s_ref.at[0])
        _, l2, a2 = _attend_blocks(q2, blocks, s_ref.at[1])
        o = a1 / l1 - lam_full * (a2 / l2)
        o = o * lax.rsqrt(jnp.mean(o * o, -1, keepdims=True) + RMS_EPS) * sub_ref[...] * (1.0 - lam_init)
        o_ref[...] = o.astype(o_ref.dtype)

    _per_query_tile(qt, k_ref.shape[0] // t, body)


def diff_prompt(proj, tiles, far, lam, subln, lam_init):
    b, s, _ = proj.shape
    t = ATT_TILE
    kern = functools.partial(_diff_prompt_kernel, lam_init=lam_init)
    return pl.pallas_call(
        kern,
        grid=(b, DIFF_HEADS, s // t),
        in_specs=[pl.BlockSpec((None, t, HEAD_DIM), lambda bi, h, qt: (bi, qt, h)),
                  pl.BlockSpec((None, s, HEAD_DIM), lambda bi, h, qt: (bi, 0, DIFF_HEADS + h)),
                  pl.BlockSpec((None, s, HEAD_DIM), lambda bi, h, qt: (bi, 0, 2 * DIFF_HEADS + h)),
                  pl.BlockSpec((None, 2, t, t), lambda bi, h, qt: (h, 0, 0, 0)),
                  pl.BlockSpec((None, 1, LANES), lambda bi, h, qt: (h, 0, 0)),
                  pl.BlockSpec(lam.shape, lambda bi, h, qt: (0, 0)),
                  pl.BlockSpec((1, HEAD_DIM), lambda bi, h, qt: (0, 0))],
        out_specs=pl.BlockSpec((None, t, HEAD_DIM), lambda bi, h, qt: (bi, qt, h)),
        out_shape=jax.ShapeDtypeStruct((b, s, DIFF_HEADS * HEAD_DIM), BF16),
        scratch_shapes=[pltpu.VMEM((s, HEAD_DIM), BF16), pltpu.VMEM((s, HEAD_DIM), BF16),
                        pltpu.VMEM((2, s // t, t, t), F32)],
        compiler_params=_params(3),
        name="diff_prompt",
    )(proj, proj, proj, tiles, far, lam, subln.reshape(1, HEAD_DIM))


def _cmp_ab_kernel(x_ref, pe_ref, w1_ref, o_ref, *, paged):
    half = w1_ref.shape[0] // 2
    hid = w1_ref.shape[1]
    per = PAGE_SIZE // NSA_CMP_STRIDE
    for g in range(NSA_KV_HEADS):
        if paged:
            n_rows = x_ref.shape[0] * per
            xg = jnp.concatenate(
                [x_ref[:, pl.ds(NSA_KV_HEADS * p + g, per, stride=NSA_KV_HEADS * NSA_CMP_STRIDE), :].reshape(n_rows, HEAD_DIM)
                 for p in range(NSA_CMP_STRIDE)], axis=1)
        else:
            xg = jnp.concatenate([x_ref[:, (NSA_KV_HEADS * p + g) * HEAD_DIM:(NSA_KV_HEADS * p + g + 1) * HEAD_DIM]
                                  for p in range(NSA_CMP_STRIDE)], axis=1)
        top = (xg + pe_ref[0:1, :]).astype(BF16)
        bot = (xg + pe_ref[1:2, :]).astype(BF16)
        o_ref[:, 2 * g * hid:(2 * g + 1) * hid] = _dot(top, w1_ref[:half, :])
        o_ref[:, (2 * g + 1) * hid:(2 * g + 2) * hid] = _dot(bot, w1_ref[half:, :])


def _cmp_fin_kernel(*refs, n_prefetch, nseq):
    ab_refs, w2_ref, o_ref = refs[n_prefetch:-2], refs[-2], refs[-1]
    per_seq = len(ab_refs) // nseq
    hid = w2_ref.shape[0]
    for s in range(nseq):
        mine = ab_refs[s * per_seq:(s + 1) * per_seq]
        ab = jnp.concatenate([r[...] for r in mine], axis=0) if per_seq > 1 else mine[0][...]
        n = ab.shape[0]
        row = lax.broadcasted_iota(jnp.int32, (n, HEAD_DIM), 0)
        for g in range(NSA_KV_HEADS):
            a = ab[:, 2 * g * hid:(2 * g + 1) * hid]
            b = ab[:, (2 * g + 1) * hid:(2 * g + 2) * hid]
            hidden = jax.nn.gelu(a + pltpu.roll(b, n - 1, 0), approximate=True)
            out = _dot(hidden.astype(BF16), w2_ref[...])
            o_ref[s, g] = jnp.where(row < n - 1, out, 0.0)


def nsa_compress(kseq, pe, w1, w2, page_table):
    rows = kseq.shape[0] * kseq.shape[1] // NSA_CMP_STRIDE
    width = NSA_CMP_STRIDE * NSA_KV_HEADS * HEAD_DIM
    hid = w1.shape[1]
    tr = math.gcd(rows, 512)
    per = PAGE_SIZE // NSA_CMP_STRIDE
    if page_table is None:
        x = kseq.reshape(rows, width)
        x_spec = pl.BlockSpec((tr, width), lambda i: (i, 0))
    else:
        x = kseq.reshape(kseq.shape[0], PAGE_SIZE * NSA_KV_HEADS, HEAD_DIM)
        x_spec = pl.BlockSpec((tr // per, PAGE_SIZE * NSA_KV_HEADS, HEAD_DIM), lambda i: (i, 0, 0))
    ab = pl.pallas_call(
        functools.partial(_cmp_ab_kernel, paged=page_table is not None),
        grid=(rows // tr,),
        in_specs=[x_spec,
                  pl.BlockSpec((2, width // 2), lambda i: (0, 0)),
                  pl.BlockSpec(w1.shape, lambda i: (0, 0))],
        out_specs=pl.BlockSpec((tr, 4 * hid), lambda i: (i, 0)),
        out_shape=jax.ShapeDtypeStruct((rows, 4 * hid), F32),
        compiler_params=_params(1),
        name="nsa_cmp_layer1",
    )(x, pe.reshape(2, width // 2), w1)
    n_cmp = 128
    if page_table is None:
        nb = kseq.shape[0]
        out_block = pl.BlockSpec((1, NSA_KV_HEADS, n_cmp, HEAD_DIM), lambda b: (b, 0, 0, 0))
        return pl.pallas_call(
            functools.partial(_cmp_fin_kernel, n_prefetch=0, nseq=1),
            grid=(nb,),
            in_specs=[pl.BlockSpec((n_cmp, 4 * hid), lambda b: (b, 0)), pl.BlockSpec(w2.shape, lambda b: (0, 0))],
            out_specs=out_block,
            out_shape=jax.ShapeDtypeStruct((nb, NSA_KV_HEADS, n_cmp, HEAD_DIM), F32),
            compiler_params=_params(1),
            name="nsa_cmp_layer2",
        )(ab, w2)
    nb, n_pages = page_table.shape
    per = PAGE_SIZE // NSA_CMP_STRIDE
    ab3 = ab.reshape(rows // per, per, 4 * hid)
    nseq = math.gcd(nb, 4)
    grid_spec = pltpu.PrefetchScalarGridSpec(
        num_scalar_prefetch=1,
        grid=(nb // nseq,),
        in_specs=[pl.BlockSpec((None, per, 4 * hid), lambda b, pt, s=s, p=p: (pt[b * nseq + s, p], 0, 0))
                  for s in range(nseq) for p in range(n_pages)]
        + [pl.BlockSpec(w2.shape, lambda b, pt: (0, 0))],
        out_specs=pl.BlockSpec((nseq, NSA_KV_HEADS, n_cmp, HEAD_DIM), lambda b, pt: (b, 0, 0, 0)),
    )
    return pl.pallas_call(
        functools.partial(_cmp_fin_kernel, n_prefetch=1, nseq=nseq),
        grid_spec=grid_spec,
        out_shape=jax.ShapeDtypeStruct((nb, NSA_KV_HEADS, n_cmp, HEAD_DIM), F32),
        compiler_params=_params(1),
        name="nsa_cmp_layer2_paged",
    )(page_table, *([ab3] * (nseq * n_pages)), w2)


def _rank_select(score, lane, n_valid_lanes, topk):
    rank = jnp.zeros(score.shape, jnp.int32)
    for j in range(n_valid_lanes):
        col = score[:, j:j + 1]
        beats = (col > score) | ((col == score) & (j < lane))
        rank = rank + beats.astype(jnp.int32)
    return rank < topk


def _nsa_cmp_kernel(q_ref, kc_ref, vc_ref, bias_ref, cov_ref, o_ref, sel_ref, *, qpos0, n_slc):
    t = q_ref.shape[1]
    qt = pl.program_id(2)
    scale = HEAD_DIM ** -0.5
    lane = lax.broadcasted_iota(jnp.int32, (t, LANES), 1)
    qpos = qpos0 + qt * t + lax.broadcasted_iota(jnp.int32, (t, LANES), 0)
    cur = qpos // NSA_SEL_BLOCK
    valid = (lane * NSA_SEL_BLOCK <= qpos) & (lane < n_slc)
    forced = valid & ((lane == 0) | (lane == cur) | (lane == cur - 1))
    for i in range(q_ref.shape[0]):
        kc = kc_ref[i].astype(BF16)
        vc = vc_ref[i].astype(BF16)
        imp = jnp.zeros((t, LANES), F32)
        for r in range(NSA_GROUP):
            qr = q_ref[i, :, r * HEAD_DIM:(r + 1) * HEAD_DIM].astype(BF16)
            bias = bias_ref[r]
            live = bias > 0.5 * NEG
            s = _dot_nt(qr, kc) * scale + bias
            m = jnp.max(s, -1, keepdims=True)
            p = jnp.where(live, jnp.exp(s - m), 0.0)
            p = p / jnp.maximum(jnp.sum(p, -1, keepdims=True), jnp.finfo(F32).tiny)
            pb = p.astype(BF16)
            o_ref[i, :, r * HEAD_DIM:(r + 1) * HEAD_DIM] = _dot(pb, vc)
            imp = imp + _dot(pb, cov_ref[...])
        score = jnp.where(forced, jnp.inf, jnp.where(valid, imp, -jnp.inf))
        chosen = _rank_select(score, lane, n_slc, min(NSA_SEL_TOPK, n_slc))
        sel_ref[i] = (valid & chosen).astype(F32)


def nsa_cmp_select(q_arr, q_col0, t, kcmp, vcmp, tab_n, qpos0, seq_len):
    b, sq, _ = q_arr.shape
    n_cmp = (seq_len - NSA_CMP_LEN) // NSA_CMP_STRIDE + 1
    n_slc = -(-seq_len // NSA_SEL_BLOCK)
    i = jnp.arange(LANES)
    qpos = qpos0 + jnp.arange(sq)
    dist = qpos[:, None] - (i * NSA_CMP_STRIDE + NSA_CMP_LEN - 1)[None, :]
    bias = _bias_of(tab_n, dist, (dist >= 0) & (i < n_cmp)[None, :]).reshape(NSA_KV_HEADS, NSA_GROUP, sq, LANES)
    cs = np.arange(LANES)[:, None] * NSA_CMP_STRIDE
    ss = np.arange(LANES)[None, :] * NSA_SEL_BLOCK
    covers = ((cs < ss + NSA_SEL_BLOCK) & (cs + NSA_CMP_LEN > ss)
              & (np.arange(LANES)[:, None] < n_cmp) & (np.arange(LANES)[None, :] < n_slc))
    covers = jnp.asarray(covers, BF16)
    kern = functools.partial(_nsa_cmp_kernel, qpos0=qpos0, n_slc=n_slc)
    gw = NSA_GROUP * HEAD_DIM
    nseq = math.gcd(b, 8) if sq == t else 1
    return pl.pallas_call(
        kern,
        grid=(b // nseq, NSA_KV_HEADS, sq // t),
        in_specs=[pl.BlockSpec((nseq, t, gw), lambda bi, g, qt: (bi, qt, q_col0 + g)),
                  pl.BlockSpec((nseq, None, LANES, HEAD_DIM), lambda bi, g, qt: (bi, g, 0, 0)),
                  pl.BlockSpec((nseq, None, LANES, HEAD_DIM), lambda bi, g, qt: (bi, g, 0, 0)),
                  pl.BlockSpec((None, NSA_GROUP, t, LANES), lambda bi, g, qt: (g, 0, qt, 0)),
                  pl.BlockSpec((LANES, LANES), lambda bi, g, qt: (0, 0))],
        out_specs=[pl.BlockSpec((nseq, t, gw), lambda bi, g, qt: (bi, qt, g)),
                   pl.BlockSpec((nseq, None, t, LANES), lambda bi, g, qt: (bi, g, qt, 0))],
        out_shape=[jax.ShapeDtypeStruct((b, sq, NSA_KV_HEADS * gw), F32),
                   jax.ShapeDtypeStruct((b, NSA_KV_HEADS, sq, LANES), F32)],
        compiler_params=_params(3),
        name="nsa_cmp_select",
    )(q_arr, kcmp, vcmp, bias, covers)


def _gate_pad(g):
    lead = g.shape[:-1]
    g = g.reshape(*lead, NSA_KV_HEADS, NSA_GROUP * 3)
    g = jnp.pad(g, [(0, 0)] * (len(lead) + 1) + [(0, LANES - NSA_GROUP * 3)])
    return g.reshape(*lead, NSA_KV_HEADS * LANES)


def _nsa_prompt_kernel(q_ref, ks_ref, vs_ref, kw_ref, vw_ref, sel_ref, ocmp_ref, g_ref, bias_ref, far_ref, e_ref,
                       o_ref, ksb_ref, vsb_ref, kwb_ref, vwb_ref, s_ref, sw_ref):
    t = q_ref.shape[0]
    qt = pl.program_id(2)
    scale = HEAD_DIM ** -0.5
    _cast_once(qt, ((ks_ref, ksb_ref), (vs_ref, vsb_ref), (kw_ref, kwb_ref), (vw_ref, vwb_ref)))
    q4 = jnp.concatenate([q_ref[:, r * HEAD_DIM:(r + 1) * HEAD_DIM] for r in range(NSA_GROUP)], axis=0).astype(BF16)
    sel4 = jnp.concatenate([sel_ref[...].astype(BF16)] * NSA_GROUP, axis=0)
    far = far_ref[...][:, :1]
    rows = NSA_GROUP * t

    def chosen(kt):
        return _dot(sel4, e_ref[:, kt * t:(kt + 1) * t]) > 0.5

    def body(n):
        slc = _causal_blocks(n, t, ksb_ref, vsb_ref,
                             lambda kt, s: jnp.where(chosen(kt), s * scale + bias_ref[0], NEG),
                             lambda kt, s: jnp.where(chosen(kt), s * scale + bias_ref[1], NEG),
                             lambda kt, s: jnp.where(chosen(kt), s * scale + far, NEG))
        _, l_s, a_s = _attend_blocks(q4, slc, s_ref)

        def edge(kt, s):
            query = lax.broadcasted_iota(jnp.int32, (rows, t), 0) & (t - 1)
            key = lax.broadcasted_iota(jnp.int32, (rows, t), 1)
            return jnp.where(key > query, s * scale + far, NEG)

        win = _causal_blocks(n, t, kwb_ref, vwb_ref,
                             lambda kt, s: s * scale + bias_ref[0],
                             lambda kt, s: s * scale + bias_ref[1],
                             edge)[max(n - 3, 0):]
        _, l_w, a_w = _attend_blocks(q4, win, sw_ref)
        o_slc = a_s / l_s
        o_win = a_w / l_w
        gate = jax.nn.sigmoid(g_ref[...])
        for r in range(NSA_GROUP):
            o = (gate[:, 3 * r:3 * r + 1] * ocmp_ref[:, r * HEAD_DIM:(r + 1) * HEAD_DIM]
                 + gate[:, 3 * r + 1:3 * r + 2] * o_slc[r * t:(r + 1) * t]
                 + gate[:, 3 * r + 2:3 * r + 3] * o_win[r * t:(r + 1) * t])
            o_ref[:, r * HEAD_DIM:(r + 1) * HEAD_DIM] = o.astype(o_ref.dtype)

    _per_query_tile(qt, ks_ref.shape[0] // t, body)


def nsa_prompt(proj, sel, o_cmp, gates, tiles, far):
    b, s, _ = proj.shape
    t = ATT_TILE
    assert 2 * t == NSA_WINDOW
    nkt = s // t
    gw = NSA_GROUP * HEAD_DIM
    kv0 = (2 * DIFF_HEADS * 2 * DIFF_DK + DIFF_HEADS * HEAD_DIM + NSA_HEADS * HEAD_DIM) // HEAD_DIM + 2 * NSA_KV_HEADS
    expand = jnp.asarray(np.arange(LANES)[:, None] == np.arange(s)[None, :] // NSA_SEL_BLOCK, BF16)
    rows = NSA_GROUP * t
    tiles4 = jnp.swapaxes(tiles.reshape(NSA_KV_HEADS, NSA_GROUP, 2, t, t), 1, 2).reshape(NSA_KV_HEADS, 2, rows, t)
    far4 = jnp.broadcast_to(far.reshape(NSA_KV_HEADS, NSA_GROUP, 1, LANES),
                            (NSA_KV_HEADS, NSA_GROUP, t, LANES)).reshape(NSA_KV_HEADS, rows, LANES)
    kv = lambda off: pl.BlockSpec((None, s, HEAD_DIM), lambda bi, g, qt: (bi, 0, kv0 + off + g))
    return pl.pallas_call(
        _nsa_prompt_kernel,
        grid=(b, NSA_KV_HEADS, nkt),
        in_specs=[pl.BlockSpec((None, t, gw), lambda bi, g, qt: (bi, qt, 6 + g)),
                  kv(0), kv(2), kv(4), kv(6),
                  pl.BlockSpec((None, None, t, LANES), lambda bi, g, qt: (bi, g, qt, 0)),
                  pl.BlockSpec((None, t, gw), lambda bi, g, qt: (bi, qt, g)),
                  pl.BlockSpec((None, t, LANES), lambda bi, g, qt: (bi, qt, g)),
                  pl.BlockSpec((None, 2, rows, t), lambda bi, g, qt: (g, 0, 0, 0)),
                  pl.BlockSpec((None, rows, LANES), lambda bi, g, qt: (g, 0, 0)),
                  pl.BlockSpec((LANES, s), lambda bi, g, qt: (0, 0))],
        out_specs=pl.BlockSpec((None, t, gw), lambda bi, g, qt: (bi, qt, g)),
        out_shape=jax.ShapeDtypeStruct((b, s, NSA_KV_HEADS * gw), BF16),
        scratch_shapes=[pltpu.VMEM((s, HEAD_DIM), BF16)] * 4
        + [pltpu.VMEM((nkt, rows, t), F32), pltpu.VMEM((min(nkt, 3), rows, t), F32)],
        compiler_params=_params(3),
        name="nsa_prompt",
    )(proj, proj, proj, proj, proj, sel, o_cmp, gates, tiles4, far4, expand)


def _moba_prompt_kernel(q_ref, k_ref, v_ref, bias_ref, far_ref, o_ref, kmean_ref, kb_ref, vb_ref, s_ref):
    t = q_ref.shape[0]
    qt = pl.program_id(2)
    nb = k_ref.shape[0] // t
    scale = HEAD_DIM ** -0.5
    _cast_once(qt, ((k_ref, kb_ref), (v_ref, vb_ref)))

    @pl.when(qt == 0)
    def _means():
        kmean_ref[...] = jnp.zeros_like(kmean_ref)
        for j in range(nb):
            kmean_ref[j:j + 1, :] = jnp.mean(k_ref[j * t:(j + 1) * t, :], axis=0, keepdims=True)

    q = q_ref[...].astype(BF16)
    lane = lax.broadcasted_iota(jnp.int32, (t, LANES), 1)
    score = jnp.where(lane < qt, _dot_nt(q, kmean_ref[...].astype(BF16)), -jnp.inf)
    sel = (lane < qt) & _rank_select(score, lane, nb, MOBA_TOPK)
    far = far_ref[...][:, :1]

    def body(n):
        blocks = _causal_blocks(n, t, kb_ref, vb_ref,
                                lambda kt, s: s * scale + bias_ref[0],
                                lambda kt, s: jnp.where(sel[:, kt:kt + 1], s * scale + bias_ref[1], NEG),
                                lambda kt, s: jnp.where(sel[:, kt:kt + 1], s * scale + far, NEG))
        _, l, acc = _attend_blocks(q, blocks, s_ref)
        o_ref[...] = (acc / l).astype(o_ref.dtype)

    _per_query_tile(qt, nb, body)


def moba_prompt(proj, tiles, far):
    b, s, _ = proj.shape
    t = ATT_TILE
    assert t == MOBA_BLOCK and s // t <= LANES
    return pl.pallas_call(
        _moba_prompt_kernel,
        grid=(b, MOBA_HEADS, s // t),
        in_specs=[pl.BlockSpec((None, t, HEAD_DIM), lambda bi, h, qt: (bi, qt, h)),
                  pl.BlockSpec((None, s, HEAD_DIM), lambda bi, h, qt: (bi, 0, MOBA_HEADS + h)),
                  pl.BlockSpec((None, s, HEAD_DIM), lambda bi, h, qt: (bi, 0, 2 * MOBA_HEADS + h)),
                  pl.BlockSpec((None, 2, t, t), lambda bi, h, qt: (h, 0, 0, 0)),
                  pl.BlockSpec((None, 1, LANES), lambda bi, h, qt: (h, 0, 0))],
        out_specs=pl.BlockSpec((None, t, HEAD_DIM), lambda bi, h, qt: (bi, qt, h)),
        out_shape=jax.ShapeDtypeStruct((b, s, MOBA_HEADS * HEAD_DIM), BF16),
        scratch_shapes=[pltpu.VMEM((LANES, HEAD_DIM), F32), pltpu.VMEM((s, HEAD_DIM), BF16),
                        pltpu.VMEM((s, HEAD_DIM), BF16), pltpu.VMEM((s // t, t, t), F32)],
        compiler_params=_params(3),
        name="moba_prompt",
    )(proj, proj, proj, tiles, far)


def _decode_bias(tab, row_t, row_col, row_kv, n_kv, qpos0, kpos, window=None):
    c = kpos.shape[0] * n_kv
    col_kv = np.arange(c) % n_kv
    kp = np.repeat(kpos, n_kv)
    dist = (qpos0 + row_t)[:, None] - kp[None, :]
    ok = (row_kv[:, None] == col_kv[None, :]) & (dist >= 0) & (kp >= 0)[None, :]
    if window is not None:
        ok = ok & (dist < window)
    per_row = jnp.take(tab.astype(F32), jnp.asarray(row_col, jnp.int32), axis=1)
    b = _lookup(per_row[:, :, None], _t5_bucket(jnp.asarray(dist)))
    return jnp.where(jnp.asarray(ok), b, NEG)


def _decode_kernel(*refs, scale, ppc, n_pages, masked, diff_lam_init):
    pt_ref, q_ref, kn_ref, vn_ref, bnew_ref, tiles_ref = refs[:6]
    pos = 6
    rm_ref = None
    if masked:
        rm_ref = refs[pos]
        pos += 1
    lam_ref = sub_ref = None
    if diff_lam_init is not None:
        lam_ref, sub_ref = refs[pos:pos + 2]
        pos += 2
    k_refs = refs[pos:pos + ppc]
    v_refs = refs[pos + ppc:pos + 2 * ppc]
    o_ref, m_ref, l_ref, acc_ref, s_ref = refs[pos + 2 * ppc:]
    single = ppc == n_pages
    c = 0 if single else pl.program_id(1)
    q = q_ref[...].astype(BF16)
    rows = q.shape[0]
    cols = k_refs[0].shape[0]

    @pl.when(pl.program_id(1) == 0)
    def _init():
        m_ref[...] = jnp.full_like(m_ref, NEG)
        l_ref[...] = jnp.zeros_like(l_ref)
        acc_ref[...] = jnp.zeros_like(acc_ref)

    s_new = _dot_nt(q, kn_ref[...].astype(BF16)) * scale + bnew_ref[...]
    if not single:
        s_new = jnp.where(c == 0, s_new, NEG)

    def page_logits(p, s):
        gp = c * ppc + p
        if single:
            tile = tiles_ref[0 if gp == 0 else (2 if gp == n_pages - 1 else 1)]
        else:
            tile = tiles_ref[1]
            if p == 0:
                tile = jnp.where(gp == 0, tiles_ref[0], tile)
            if p == ppc - 1:
                tile = jnp.where(gp == n_pages - 1, tiles_ref[2], tile)
        s = s * scale + tile
        if masked:
            assert single
            col = lax.broadcasted_iota(jnp.int32, (rows, cols), 1)
            keep = jnp.where(col < cols // 2, rm_ref[:, 2 * gp:2 * gp + 1], rm_ref[:, 2 * gp + 1:2 * gp + 2])
            s = jnp.where(keep > 0.5, s, NEG)
        return s

    blocks = [(lambda p=p: k_refs[p][...].astype(BF16), lambda p=p: v_refs[p][...].astype(BF16),
               functools.partial(page_logits, p)) for p in range(ppc)]
    m_old = m_ref[...]
    m, l_pages, acc_pages = _attend_blocks(q, blocks, s_ref,
                                           m_floor=jnp.maximum(m_old, jnp.max(s_new, -1, keepdims=True)))
    alpha = jnp.exp(m_old - m)
    p_new = jnp.exp(s_new - m)
    l_ref[...] = alpha * l_ref[...] + l_pages + jnp.sum(p_new, -1, keepdims=True)
    acc_ref[...] = alpha * acc_ref[...] + acc_pages + _dot(p_new.astype(BF16), vn_ref[...].astype(BF16))
    m_ref[...] = m

    @pl.when(pl.program_id(1) == pl.num_programs(1) - 1)
    def _finish():
        o = acc_ref[...] / l_ref[...]
        if diff_lam_init is not None:
            lam = lam_ref[...]
            lam_full = (jnp.exp(jnp.sum(lam[0:1] * lam[1:2], keepdims=True))
                        - jnp.exp(jnp.sum(lam[2:3] * lam[3:4], keepdims=True)) + diff_lam_init)
            half = rows // 2
            o = o[:half] - lam_full * o[half:]
            o = o * lax.rsqrt(jnp.mean(o * o, -1, keepdims=True) + RMS_EPS) * sub_ref[...] * (1.0 - diff_lam_init)
        o_ref[...] = o.astype(o_ref.dtype)


def decode_attn(page_table, q, k_new, v_new, bias_new, tiles, k_pages, v_pages, n_chunks, scale, out_dtype,
                row_mask=None, diff=None, name="decode_attn"):
    bs, r, _ = q.shape
    n_pages = page_table.shape[1]
    ppc = n_pages // n_chunks
    c = k_pages.shape[1]
    per_b = lambda shape: pl.BlockSpec((None,) + shape, lambda b, ch, pt: (b, 0, 0))
    const = lambda shape: pl.BlockSpec(shape, lambda b, ch, pt: (0,) * len(shape))
    in_specs = [per_b((r, HEAD_DIM)), per_b((LANES, HEAD_DIM)), per_b((LANES, HEAD_DIM)),
                const((r, LANES)), const((3, r, c))]
    args = [q, k_new, v_new, bias_new, tiles]
    if row_mask is not None:
        in_specs.append(per_b((r, LANES)))
        args.append(row_mask)
    r_out = r
    if diff is not None:
        in_specs += [const(diff[0].shape), const((1, HEAD_DIM))]
        args += [diff[0], diff[1].reshape(1, HEAD_DIM)]
        r_out = r // 2
    page = lambda p: pl.BlockSpec((None, c, HEAD_DIM), lambda b, ch, pt: (pt[b, ch * ppc + p], 0, 0))
    in_specs += [page(p) for p in range(ppc)] * 2
    args += [k_pages] * ppc + [v_pages] * ppc
    kern = functools.partial(_decode_kernel, scale=scale, ppc=ppc, n_pages=n_pages, masked=row_mask is not None,
                             diff_lam_init=None if diff is None else diff[2])
    grid_spec = pltpu.PrefetchScalarGridSpec(
        num_scalar_prefetch=1,
        grid=(bs, n_chunks),
        in_specs=in_specs,
        out_specs=pl.BlockSpec((None, r_out, HEAD_DIM), lambda b, ch, pt: (b, 0, 0)),
        scratch_shapes=[pltpu.VMEM((r, 1), F32), pltpu.VMEM((r, 1), F32), pltpu.VMEM((r, HEAD_DIM), F32),
                        pltpu.VMEM((ppc, r, c), F32)],
    )
    return pl.pallas_call(
        kern,
        grid_spec=grid_spec,
        out_shape=jax.ShapeDtypeStruct((bs, r_out, HEAD_DIM), out_dtype),
        compiler_params=_params(2),
        name=name,
    )(page_table, *args)


def _pad_rows(x, n):
    return jnp.pad(x, ((0, 0), (0, n - x.shape[1]), (0, 0)))


def _moba_decode_kernel(*refs, bps):
    pt_ref, q_ref, kn_ref, vn_ref, bnew_ref, tiles_ref = refs[:6]
    k_refs = refs[6:6 + 2 * bps]
    v_refs = refs[6 + 2 * bps:6 + 4 * bps]
    o_ref, ssel_ref, mb_ref, lb_ref, ab_ref, s_ref = refs[6 + 4 * bps:]
    j = pl.program_id(1)
    nb = pl.num_programs(1)
    n_blk = mb_ref.shape[0]
    q = q_ref[...].astype(BF16)
    rows = q.shape[0]
    scale = HEAD_DIM ** -0.5
    lane = lax.broadcasted_iota(jnp.int32, (rows, LANES), 1)

    @pl.when(j == 0)
    def _init():
        ssel_ref[...] = jnp.full_like(ssel_ref, -jnp.inf)

    for i in range(bps):
        blk = j * bps + i
        k0_ref, k1_ref, v0_ref, v1_ref = k_refs[2 * i], k_refs[2 * i + 1], v_refs[2 * i], v_refs[2 * i + 1]
        ksum = (jnp.sum(k0_ref[...].reshape(PAGE_SIZE, MOBA_HEADS, HEAD_DIM), axis=0)
                + jnp.sum(k1_ref[...].reshape(PAGE_SIZE, MOBA_HEADS, HEAD_DIM), axis=0))
        last = tiles_ref[0]
        if i == bps - 1:
            last = jnp.where(j == nb - 1, tiles_ref[1], last)
        blocks = [(lambda r=k0_ref: r[...].astype(BF16), lambda r=v0_ref: r[...].astype(BF16),
                   lambda s: s * scale + tiles_ref[0]),
                  (lambda r=k1_ref: r[...].astype(BF16), lambda r=v1_ref: r[...].astype(BF16),
                   lambda s, last=last: s * scale + last)]
        m_blk, l_blk, acc_blk = _attend_blocks(q, blocks, s_ref.at[i])
        mb_ref[blk] = m_blk
        lb_ref[blk] = l_blk
        ab_ref[blk] = acc_blk
        kmean = (ksum * (1.0 / MOBA_BLOCK)).astype(BF16).astype(F32)
        kmean_rows = jnp.concatenate([kmean] * (rows // MOBA_HEADS), axis=0)
        score = jnp.sum(q.astype(F32) * kmean_rows, -1, keepdims=True)
        ssel_ref[...] = jnp.where(lane == blk, score, ssel_ref[...])

    @pl.when(j == nb - 1)
    def _finish():
        chosen = (lane < n_blk) & _rank_select(ssel_ref[...], lane, n_blk, MOBA_TOPK)
        sel = chosen.astype(F32)
        s_new = _dot_nt(q, kn_ref[...].astype(BF16)) * scale + bnew_ref[...]
        keep = [jnp.sum(jnp.where(lane == b, sel, 0.0), -1, keepdims=True) > 0.5 for b in range(n_blk)]
        m_tot = jnp.max(s_new, -1, keepdims=True)
        for b in range(n_blk):
            m_tot = jnp.maximum(m_tot, jnp.where(keep[b], mb_ref[b], NEG))
        p_new = jnp.exp(s_new - m_tot)
        l_tot = jnp.sum(p_new, -1, keepdims=True)
        acc = _dot(p_new.astype(BF16), vn_ref[...].astype(BF16))
        for b in range(n_blk):
            w = jnp.where(keep[b], jnp.exp(mb_ref[b] - m_tot), 0.0)
            l_tot = l_tot + w * lb_ref[b]
            acc = acc + w * ab_ref[b]
        o_ref[...] = (acc / l_tot).astype(o_ref.dtype)


def moba_decode(page_table, q, k_new, v_new, bias_new, tiles, k_pages, v_pages):
    bs, r, _ = q.shape
    n_pages = page_table.shape[1]
    ppb = MOBA_BLOCK // PAGE_SIZE
    assert ppb == 2 and n_pages % ppb == 0 and n_pages // ppb >= MOBA_TOPK
    n_blk = n_pages // ppb
    bps = math.gcd(n_blk, 4)
    c = k_pages.shape[1]
    per_b = lambda shape: pl.BlockSpec((None,) + shape, lambda b, j, pt: (b, 0, 0))
    const = lambda shape: pl.BlockSpec(shape, lambda b, j, pt: (0,) * len(shape))
    page = lambda p: pl.BlockSpec((None, c, HEAD_DIM), lambda b, j, pt: (pt[b, ppb * bps * j + p], 0, 0))
    pages = [page(p) for p in range(ppb * bps)]
    grid_spec = pltpu.PrefetchScalarGridSpec(
        num_scalar_prefetch=1,
        grid=(bs, n_blk // bps),
        in_specs=[per_b((r, HEAD_DIM)), per_b((LANES, HEAD_DIM)), per_b((LANES, HEAD_DIM)),
                  const((r, LANES)), const((2, r, c))] + pages + pages,
        out_specs=pl.BlockSpec((None, r, HEAD_DIM), lambda b, j, pt: (b, 0, 0)),
        scratch_shapes=[pltpu.VMEM((r, LANES), F32), pltpu.VMEM((n_blk, r, 1), F32), pltpu.VMEM((n_blk, r, 1), F32),
                        pltpu.VMEM((n_blk, r, HEAD_DIM), F32), pltpu.VMEM((bps, ppb, r, c), F32)],
    )
    return pl.pallas_call(
        functools.partial(_moba_decode_kernel, bps=bps),
        grid_spec=grid_spec,
        out_shape=jax.ShapeDtypeStruct((bs, r, HEAD_DIM), BF16),
        compiler_params=_params(2),
        name="moba_decode",
    )(page_table, q, k_new, v_new, bias_new, tiles, *([k_pages] * (ppb * bps)), *([v_pages] * (ppb * bps)))


def _nsa_combine_kernel(oc_ref, os_ref, ow_ref, g_ref, o_ref):
    gate = jax.nn.sigmoid(g_ref[...])
    o = gate[..., 0:1] * oc_ref[...] + gate[..., 1:2] * os_ref[...] + gate[..., 2:3] * ow_ref[...]
    o_ref[...] = o.astype(o_ref.dtype)


def nsa_combine(o_cmp, o_slc, o_win, gates):
    full = lambda a: pl.BlockSpec(a.shape, lambda: (0,) * a.ndim)
    return pl.pallas_call(
        _nsa_combine_kernel,
        in_specs=[full(o_cmp), full(o_slc), full(o_win), full(gates)],
        out_specs=full(o_cmp),
        out_shape=jax.ShapeDtypeStruct(o_cmp.shape, BF16),
        compiler_params=pltpu.CompilerParams(vmem_limit_bytes=VMEM_LIMIT),
        name="nsa_combine",
    )(o_cmp, o_slc, o_win, gates)


AB_MAIN = 2 * DIFF_HEADS * 2 * DIFF_DK + DIFF_HEADS * HEAD_DIM + NSA_HEADS * HEAD_DIM + 6 * NSA_KV_HEADS * HEAD_DIM
ROW_TILE = 512


def _in_proj(h, w, tn):
    m = h.shape[0]
    n = w.shape[1]
    n_main = n // tn * tn
    tm = min(m, 2 * ROW_TILE)
    main = matmul(h, w, n_main, tm, tn, "in_proj")
    if n_main == n:
        return main, None
    tail = jnp.pad(w[:, n_main:], ((0, 0), (0, LANES - (n - n_main))))
    return main, matmul(h, tail, LANES, tm, LANES, "in_proj_tail")[:, :n - n_main]


def _even_prompt(proj, gates, b, s, lam, subln, lam_init, pe, w1, w2, tab, tiles, far):
    p3 = proj.reshape(b, s, AB_MAIN)
    o_diff = diff_prompt(p3, tiles[:DIFF_HEADS], far[:DIFF_HEADS], lam, subln, lam_init)
    kv = NSA_KV_HEADS * HEAD_DIM
    c0 = AB_MAIN - 6 * kv
    kc, vc, ks, vs, kw, vw = [p3[..., c0 + i * kv:c0 + (i + 1) * kv] for i in range(6)]
    tab_n = tab[:, DIFF_HEADS:]
    kcmp = nsa_compress(kc, pe[0], w1[0], w2[0], None)
    vcmp = nsa_compress(vc, pe[1], w1[1], w2[1], None)
    o_cmp, sel = nsa_cmp_select(p3, 6, ATT_TILE, kcmp, vcmp, tab_n, 0, s)
    o_nsa = nsa_prompt(p3, sel, o_cmp, _gate_pad(gates).reshape(b, s, NSA_KV_HEADS * LANES),
                       tiles[DIFF_HEADS:], far[DIFF_HEADS:])
    hd = DIFF_HEADS * HEAD_DIM
    keep = min(NSA_WINDOW, s)
    state = (p3[..., hd:2 * hd].reshape(b, s, DIFF_HEADS, HEAD_DIM),
             p3[..., 2 * hd:3 * hd].reshape(b, s, DIFF_HEADS, HEAD_DIM),
             *[a.reshape(b, s, NSA_KV_HEADS, HEAD_DIM) for a in (kc, vc, ks, vs)],
             kw[:, s - keep:].reshape(b, keep, NSA_KV_HEADS, HEAD_DIM),
             vw[:, s - keep:].reshape(b, keep, NSA_KV_HEADS, HEAD_DIM))
    return o_diff.reshape(b * s, hd), o_nsa.reshape(b * s, NSA_HEADS * HEAD_DIM), state


def _even_sample(proj, gates, bs, ts, page_table, cache_dk, cache_dv, cache_ck, cache_cv, cache_sk, cache_sv,
                 win_k, win_v, lam, subln, lam_init, pe, w1, w2, tab):
    n_pool = cache_dk.shape[0]
    n_pages = page_table.shape[1]
    past = n_pages * PAGE_SIZE
    hd = DIFF_HEADS * HEAD_DIM
    kv = NSA_KV_HEADS * HEAD_DIM
    p3 = proj.reshape(bs, ts, AB_MAIN)
    dq, dk, dv, nq = [p3[..., i * hd:(i + 1) * hd] for i in range(4)]
    c0 = AB_MAIN - 6 * kv
    kc, vc, ks, vs, kw, vw = [p3[..., c0 + i * kv:c0 + (i + 1) * kv] for i in range(6)]
    tab_d, tab_n = tab[:, :DIFF_HEADS], tab[:, DIFF_HEADS:]
    new_pos = np.where(np.arange(LANES) < ts, past + np.arange(LANES), -1)

    r = ts * DIFF_HEADS
    q = dq.reshape(bs, r, HEAD_DIM)
    lane = jnp.arange(HEAD_DIM)
    q2 = jnp.concatenate([jnp.where(lane < DIFF_DK, q, 0.0), jnp.where(lane >= DIFF_DK, q, 0.0)], axis=1)
    row_t = np.tile(np.repeat(np.arange(ts), DIFF_HEADS), 2)
    row_h = np.tile(np.arange(DIFF_HEADS), 2 * ts)
    page_pos = lambda p: p * PAGE_SIZE + np.arange(PAGE_SIZE)
    d_bias = lambda kpos: _decode_bias(tab_d, row_t, row_h, row_h, DIFF_HEADS, past, kpos)
    tiles_d = jnp.stack([d_bias(page_pos(p)) for p in (0, 1, n_pages - 1)])
    o_diff = decode_attn(page_table, q2, _pad_rows(dk.reshape(bs, r, HEAD_DIM), LANES),
                         _pad_rows(dv.reshape(bs, r, HEAD_DIM), LANES), d_bias(new_pos[:LANES // DIFF_HEADS]), tiles_d,
                         cache_dk.reshape(n_pool, PAGE_SIZE * DIFF_HEADS, HEAD_DIM),
                         cache_dv.reshape(n_pool, PAGE_SIZE * DIFF_HEADS, HEAD_DIM),
                         1, DIFF_DK ** -0.5, BF16, diff=(lam, subln, lam_init), name="diff_decode")

    rn = ts * NSA_HEADS
    qn = nq.reshape(bs, rn, HEAD_DIM)
    row_tn = np.repeat(np.arange(ts), NSA_HEADS)
    row_cn = np.tile(np.arange(NSA_HEADS), ts)
    row_gn = row_cn // NSA_GROUP
    kcmp = nsa_compress(cache_ck, pe[0], w1[0], w2[0], page_table)
    vcmp = nsa_compress(cache_cv, pe[1], w1[1], w2[1], page_table)
    o_cmp, sel = nsa_cmp_select(p3, 6, ts, kcmp, vcmp, tab_n, past, past + ts)
    row_mask = jnp.broadcast_to(jnp.swapaxes(sel, 1, 2)[:, :, :, None, :], (bs, ts, NSA_KV_HEADS, NSA_GROUP, LANES))
    n_bias = lambda kpos, window=None: _decode_bias(tab_n, row_tn, row_cn, row_gn, NSA_KV_HEADS, past, kpos, window)
    new_n = new_pos[:LANES // NSA_KV_HEADS]
    k_new = lambda a: _pad_rows(a.reshape(bs, ts * NSA_KV_HEADS, HEAD_DIM), LANES)
    tiles_s = jnp.stack([n_bias(page_pos(p)) for p in (0, 1, n_pages - 1)])
    o_slc = decode_attn(page_table, qn, k_new(ks), k_new(vs), n_bias(new_n), tiles_s,
                        cache_sk.reshape(n_pool, PAGE_SIZE * NSA_KV_HEADS, HEAD_DIM),
                        cache_sv.reshape(n_pool, PAGE_SIZE * NSA_KV_HEADS, HEAD_DIM),
                        1, HEAD_DIM ** -0.5, F32, row_mask=row_mask.reshape(bs, rn, LANES), name="nsa_slc_decode")
    w_buf = win_k.shape[1]
    w_pages = w_buf // PAGE_SIZE
    win_table = jnp.arange(bs * w_pages, dtype=jnp.int32).reshape(bs, w_pages)
    win_pos = lambda p: past - w_buf + p * PAGE_SIZE + np.arange(PAGE_SIZE)
    tiles_w = jnp.stack([n_bias(win_pos(p), NSA_WINDOW) for p in (0, 1, w_pages - 1)])
    o_win = decode_attn(win_table, qn, k_new(kw), k_new(vw), n_bias(new_n, NSA_WINDOW), tiles_w,
                        win_k.reshape(bs * w_pages, PAGE_SIZE * NSA_KV_HEADS, HEAD_DIM),
                        win_v.reshape(bs * w_pages, PAGE_SIZE * NSA_KV_HEADS, HEAD_DIM),
                        1, HEAD_DIM ** -0.5, F32, name="nsa_win_decode")
    o_nsa = nsa_combine(o_cmp.reshape(bs, rn, HEAD_DIM), o_slc, o_win, gates.reshape(bs, rn, 3))

    as_kv = lambda a: a.reshape(bs, ts, NSA_KV_HEADS, HEAD_DIM)
    keep = min(NSA_WINDOW, past + ts)
    state = (dk.reshape(bs, ts, DIFF_HEADS, HEAD_DIM), dv.reshape(bs, ts, DIFF_HEADS, HEAD_DIM),
             as_kv(kc), as_kv(vc), as_kv(ks), as_kv(vs),
             jnp.concatenate([win_k, as_kv(kw)], axis=1)[:, w_buf + ts - keep:],
             jnp.concatenate([win_v, as_kv(vw)], axis=1)[:, w_buf + ts - keep:])
    return o_diff.reshape(bs * ts, hd), o_nsa.reshape(bs * ts, NSA_HEADS * HEAD_DIM), state


def _odd_sample(proj, bs, ts, page_table, cache_k, cache_v, tab):
    n_pool = cache_k.shape[0]
    n_pages = page_table.shape[1]
    past = n_pages * PAGE_SIZE
    hd = MOBA_HEADS * HEAD_DIM
    p3 = proj.reshape(bs, ts, 3 * hd)
    q, k, v = [p3[..., i * hd:(i + 1) * hd] for i in range(3)]
    r = ts * MOBA_HEADS
    row_t = np.repeat(np.arange(ts), MOBA_HEADS)
    row_h = np.tile(np.arange(MOBA_HEADS), ts)
    bias = lambda kpos: _decode_bias(tab, row_t, row_h, row_h, MOBA_HEADS, past, kpos)
    page_pos = lambda p: p * PAGE_SIZE + np.arange(PAGE_SIZE)
    new_pos = np.where(np.arange(LANES // MOBA_HEADS) < ts, past + np.arange(LANES // MOBA_HEADS), -1)
    tiles = jnp.stack([bias(page_pos(1)), bias(page_pos(n_pages - 1))])
    o = moba_decode(page_table, q.reshape(bs, r, HEAD_DIM), _pad_rows(k.reshape(bs, r, HEAD_DIM), LANES),
                    _pad_rows(v.reshape(bs, r, HEAD_DIM), LANES), bias(new_pos), tiles,
                    cache_k.reshape(n_pool, PAGE_SIZE * MOBA_HEADS, HEAD_DIM),
                    cache_v.reshape(n_pool, PAGE_SIZE * MOBA_HEADS, HEAD_DIM))
    state = (k.reshape(bs, ts, MOBA_HEADS, HEAD_DIM), v.reshape(bs, ts, MOBA_HEADS, HEAD_DIM))
    return o.reshape(bs * ts, hd), state


def kernel(x_prompt, x_sample, c_prompt, c_sample, page_table, cache_diff_k, cache_diff_v, cache_nsa_cmp_k, cache_nsa_cmp_v, cache_nsa_slc_k, cache_nsa_slc_v, state_nsa_win_k, state_nsa_win_v, cache_moba_k, cache_moba_v, w_in_ab, w_out_ab, diff_lambda, diff_subln_g, nsa_cmp_pe, nsa_cmp_w1, nsa_cmp_w2, w_in_c, w_out_c, rel_bias, ada_w, ada_b, ln_g, ln_b, router_group_w, router_group_b, router_expert_w, router_expert_b, moe_w_gate, moe_w_up, moe_w_down):
    b, s, d = x_prompt.shape
    bs, ts, _ = x_sample.shape
    mp, ms = b * s, bs * ts
    depth = ada_w.shape[0]
    mod = ada_mod(jnp.concatenate([c_prompt, c_sample]), ada_w.reshape(2 * depth, d, 3 * d), ada_b.reshape(2 * depth, 3 * d))

    def mod_vectors(sub):
        per_tok = jnp.repeat(mod[sub, b:], ts, axis=0)
        return ([mod[sub, :b, None, i * d:(i + 1) * d] for i in range(3)],
                [per_tok[None, :, i * d:(i + 1) * d] for i in range(3)])

    xp, xs = x_prompt.reshape(mp, d), x_sample.reshape(ms, d)
    tiles, far = _toeplitz_tiles(rel_bias, ATT_TILE), _far_bias(rel_bias)
    even_p, even_s, odd_p, odd_s = [], [], [], []
    for l in range(depth):
        i = l // 2
        (shp, scp, gp), (shs, scs, gs) = mod_vectors(2 * l)
        hp = mod_cast(xp, shp, scp, ROW_TILE)
        hs = mod_cast(xs, shs, scs, ms)
        if l % 2 == 0:
            lam_init = 0.8 - 0.6 * math.exp(-0.3 * l)
            w1, w2 = nsa_cmp_w1[i].astype(BF16), nsa_cmp_w2[i].astype(BF16)
            proj_p, gates_p = _in_proj(hp, w_in_ab[i], 512)
            proj_s, gates_s = _in_proj(hs, w_in_ab[i], 512)
            oa_p, ob_p, st_p = _even_prompt(proj_p, gates_p, b, s, diff_lambda[i], diff_subln_g[i], lam_init,
                                            nsa_cmp_pe[i], w1, w2, rel_bias, tiles, far)
            oa_s, ob_s, st_s = _even_sample(proj_s, gates_s, bs, ts, page_table, cache_diff_k[i], cache_diff_v[i],
                                            cache_nsa_cmp_k[i], cache_nsa_cmp_v[i], cache_nsa_slc_k[i],
                                            cache_nsa_slc_v[i], state_nsa_win_k[i], state_nsa_win_v[i],
                                            diff_lambda[i], diff_subln_g[i], lam_init, nsa_cmp_pe[i], w1, w2, rel_bias)
            even_p.append(st_p)
            even_s.append(st_s)
            w_out = w_out_ab[i].astype(BF16)
            cols = (0, 0)
        else:
            proj_p, _ = _in_proj(hp, w_in_c[i], 1024)
            proj_s, _ = _in_proj(hs, w_in_c[i], 1024)
            hd = MOBA_HEADS * HEAD_DIM
            p3 = proj_p.reshape(b, s, 3 * hd)
            oa_p = ob_p = moba_prompt(p3, tiles, far).reshape(mp, hd)
            odd_p.append((p3[..., hd:2 * hd].reshape(b, s, MOBA_HEADS, HEAD_DIM),
                          p3[..., 2 * hd:].reshape(b, s, MOBA_HEADS, HEAD_DIM)))
            oa_s, st_s = _odd_sample(proj_s, bs, ts, page_table, cache_moba_k[i], cache_moba_v[i], rel_bias)
            ob_s = oa_s
            odd_s.append(st_s)
            w_out = w_out_c[i].astype(BF16)
            cols = (0, 1)
        xp = out_proj_norm(oa_p, ob_p, cols[0], cols[1], w_out, xp, gp, ln_g[l, 0], ln_b[l, 0], 256)
        xs = out_proj_norm(oa_s, ob_s, cols[0], cols[1], w_out, xs, gs, ln_g[l, 0], ln_b[l, 0], ms)

        (shp, scp, gp), (shs, scs, gs) = mod_vectors(2 * l + 1)
        hp = mod_cast(xp, shp, scp, ROW_TILE)
        hs = mod_cast(xs, shs, scs, ms)
        rw, rb = _router_pack(router_group_w[l], router_group_b[l], router_expert_w[l], router_expert_b[l])
        wg, wu, wd = cast_bf16(moe_w_gate[l]), cast_bf16(moe_w_up[l]), cast_bf16(moe_w_down[l])
        xp = moe_norm(hp, rw, rb, wg, wu, wd, xp, gp, ln_g[l, 1], ln_b[l, 1], ROW_TILE)
        xs = moe_norm(hs, rw, rb, wg, wu, wd, xs, gs, ln_g[l, 1], ln_b[l, 1], ms)

    stack = lambda rows, j: jnp.stack([r[j] for r in rows])
    outs = [xp.reshape(b, s, d), xs.reshape(bs, ts, d)]
    for j in range(8):
        outs += [stack(even_p, j), stack(even_s, j)]
    for j in range(2):
        outs += [stack(odd_p, j), stack(odd_s, j)]
    return tuple(outs)
```

```python
import functools
import math

import numpy as np
import jax
import jax.numpy as jnp
from jax import lax
from jax.experimental import pallas as pl
from jax.experimental.pallas import tpu as pltpu

F32 = jnp.float32
BF16 = jnp.bfloat16
NEG = -1e30

HEAD_DIM = 128
DIFF_HEADS = 8
DIFF_DK = 64
NSA_HEADS = 8
NSA_KV_HEADS = 2
NSA_GROUP = 4
NSA_CMP_LEN = 32
NSA_CMP_STRIDE = 16
NSA_CMP_HIDDEN = 256
NSA_SEL_BLOCK = 64
NSA_SEL_TOPK = 16
NSA_WINDOW = 512
MOBA_HEADS = 16
MOBA_BLOCK = 256
MOBA_TOPK = 3
PAGE_SIZE = 128
REL_BUCKETS = 32
REL_MAX_DIST = 128
N_GROUPS = 4
EXPERTS_PER_GROUP = 4
N_EXPERTS = 16
DEPTH = 2
DEEPNORM_ALPHA = (2 * DEPTH) ** 0.25
LN_EPS = 1e-5
RMS_EPS = 1e-6

LANES = 128
ATT_TILE = 256
VMEM_LIMIT = 56 * 1024 * 1024


def _params(n_axes, vmem=VMEM_LIMIT):
    return pltpu.CompilerParams(dimension_semantics=("arbitrary",) * n_axes, vmem_limit_bytes=vmem)


def _dot(a, b):
    return jnp.dot(a, b, preferred_element_type=F32)


def _dot_nt(a, b):
    return lax.dot_general(a, b, (((1,), (1,)), ((), ())), preferred_element_type=F32)


def _ada_kernel(c_ref, w_ref, b_ref, o_ref):
    c = c_ref[...]
    a = (c * jax.nn.sigmoid(c)).astype(BF16)
    o_ref[...] = _dot(a, w_ref[...].astype(BF16)) + b_ref[...]


def ada_mod(c_all, ada_w, ada_b):
    n_sub, d, d3 = ada_w.shape
    bc = c_all.shape[0]
    tn = 768
    return pl.pallas_call(
        _ada_kernel,
        grid=(n_sub, d3 // tn),
        in_specs=[pl.BlockSpec((bc, d), lambda s, j: (0, 0)),
                  pl.BlockSpec((None, d, tn), lambda s, j: (s, 0, j)),
                  pl.BlockSpec((None, 1, tn), lambda s, j: (s, 0, j))],
        out_specs=pl.BlockSpec((None, bc, tn), lambda s, j: (s, 0, j)),
        out_shape=jax.ShapeDtypeStruct((n_sub, bc, d3), F32),
        compiler_params=_params(2),
        name="ada_mod",
    )(c_all, ada_w, ada_b.reshape(n_sub, 1, d3))


def _seg_spec(vec, m, tm):
    g, tg, d = vec.shape
    if tg == 1:
        per = m // g // tm
        return pl.BlockSpec((None, 1, d), lambda i, *_: (i // per, 0, 0))
    return pl.BlockSpec((None, tm, d), lambda i, *_: (0, i, 0))


def _mod_kernel(x_ref, sh_ref, sc_ref, o_ref):
    o_ref[...] = (x_ref[...] * (1 + sc_ref[...]) + sh_ref[...]).astype(BF16)


def mod_cast(x, shift, scale, tm):
    m, d = x.shape
    return pl.pallas_call(
        _mod_kernel,
        grid=(m // tm,),
        in_specs=[pl.BlockSpec((tm, d), lambda i: (i, 0)), _seg_spec(shift, m, tm), _seg_spec(scale, m, tm)],
        out_specs=pl.BlockSpec((tm, d), lambda i: (i, 0)),
        out_shape=jax.ShapeDtypeStruct((m, d), BF16),
        compiler_params=_params(1),
        name="mod_cast",
    )(x, shift, scale)


def _mm_kernel(a_ref, w_ref, o_ref, wb_ref):
    @pl.when(pl.program_id(1) == 0)
    def _round_weights():
        wb_ref[...] = w_ref[...].astype(BF16)

    o_ref[...] = _dot(a_ref[...], wb_ref[...]).astype(o_ref.dtype)


def matmul(a, w, n_cols, tm, tn, name, col0=0):
    m, k = a.shape
    assert col0 % tn == 0 and n_cols % tn == 0
    first = col0 // tn
    return pl.pallas_call(
        _mm_kernel,
        grid=(n_cols // tn, m // tm),
        in_specs=[pl.BlockSpec((tm, k), lambda j, i: (i, 0)), pl.BlockSpec((k, tn), lambda j, i: (0, first + j))],
        out_specs=pl.BlockSpec((tm, tn), lambda j, i: (i, j)),
        out_shape=jax.ShapeDtypeStruct((m, n_cols), F32),
        scratch_shapes=[pltpu.VMEM((k, tn), BF16)],
        compiler_params=_params(2),
        name=name,
    )(a, w)


def _cast_kernel(x_ref, o_ref):
    o_ref[...] = x_ref[...].astype(o_ref.dtype)


def cast_bf16(w):
    e, r, c = w.shape
    return pl.pallas_call(
        _cast_kernel,
        grid=(e,),
        in_specs=[pl.BlockSpec((None, r, c), lambda i: (i, 0, 0))],
        out_specs=pl.BlockSpec((None, r, c), lambda i: (i, 0, 0)),
        out_shape=jax.ShapeDtypeStruct(w.shape, BF16),
        compiler_params=_params(1),
        name="cast_bf16",
    )(w)


def _postnorm(x, y, gate, g, b):
    z = DEEPNORM_ALPHA * x + (1 + gate) * y
    zc = z - jnp.mean(z, -1, keepdims=True)
    var = jnp.mean(zc * zc, -1, keepdims=True)
    return zc * lax.rsqrt(var + LN_EPS) * g + b


def _outln_kernel(oa_ref, ob_ref, w_ref, x_ref, gate_ref, g_ref, b_ref, out_ref):
    half = oa_ref.shape[1]
    y = _dot(oa_ref[...], w_ref[:half, :]) + _dot(ob_ref[...], w_ref[half:, :])
    out_ref[...] = _postnorm(x_ref[...], y, gate_ref[...], g_ref[...], b_ref[...])


def out_proj_norm(oa, ob, cols_a, cols_b, w_out, x, gate, ln_g, ln_b, tm):
    m, d = x.shape
    half = w_out.shape[0] // 2
    return pl.pallas_call(
        _outln_kernel,
        grid=(m // tm,),
        in_specs=[pl.BlockSpec((tm, half), lambda i: (i, cols_a)),
                  pl.BlockSpec((tm, half), lambda i: (i, cols_b)),
                  pl.BlockSpec(w_out.shape, lambda i: (0, 0)),
                  pl.BlockSpec((tm, d), lambda i: (i, 0)),
                  _seg_spec(gate, m, tm),
                  pl.BlockSpec((1, d), lambda i: (0, 0)),
                  pl.BlockSpec((1, d), lambda i: (0, 0))],
        out_specs=pl.BlockSpec((tm, d), lambda i: (i, 0)),
        out_shape=jax.ShapeDtypeStruct((m, d), F32),
        compiler_params=_params(1),
        name="out_proj_norm",
    )(oa, ob, w_out, x, gate, ln_g.reshape(1, d), ln_b.reshape(1, d))


def _first_argmax(v, lane, valid):
    vm = jnp.where(valid, v, -jnp.inf)
    top = jnp.max(vm, -1, keepdims=True)
    idx = jnp.min(jnp.where(valid & (vm == top), lane, 1 << 20), -1, keepdims=True)
    return top, idx


def _moe_kernel(h_ref, rw_ref, rb_ref, wg_ref, wu_ref, wd_ref, x_ref, gate_ref, g_ref, b_ref, out_ref,
                comb_ref, acc_ref):
    e = pl.program_id(1)
    h = h_ref[...]

    @pl.when(e == 0)
    def _route():
        logits = _dot(h, rw_ref[...]) + rb_ref[...]
        lane = lax.broadcasted_iota(jnp.int32, logits.shape, 1)
        is_g = lane < N_GROUPS
        gmax = jnp.max(jnp.where(is_g, logits, -jnp.inf), -1, keepdims=True)
        gex = jnp.where(is_g, jnp.exp(logits - gmax), 0.0)
        pg = gex / jnp.sum(gex, -1, keepdims=True)
        pg_top, g_top = _first_argmax(pg, lane, is_g)
        ex_id = lane - N_GROUPS
        in_grp = (ex_id >= g_top * EXPERTS_PER_GROUP) & (ex_id < (g_top + 1) * EXPERTS_PER_GROUP)
        emax = jnp.max(jnp.where(in_grp, logits, -jnp.inf), -1, keepdims=True)
        eex = jnp.where(in_grp, jnp.exp(logits - emax), 0.0)
        pe = eex / jnp.sum(eex, -1, keepdims=True)
        p1, i1 = _first_argmax(pe, lane, in_grp)
        p2, i2 = _first_argmax(pe, lane, in_grp & (lane != i1))
        tot = p1 + p2
        comb = jnp.where(lane == i1, p1 / tot * pg_top, 0.0) + jnp.where(lane == i2, p2 / tot * pg_top, 0.0)
        comb_ref[...] = comb
        acc_ref[...] = jnp.zeros_like(acc_ref)

    lane = lax.broadcasted_iota(jnp.int32, comb_ref.shape, 1)
    c_e = jnp.sum(jnp.where(lane == e + N_GROUPS, comb_ref[...], 0.0), -1, keepdims=True)
    a = _dot(h, wg_ref[...])
    u = _dot(h, wu_ref[...])
    hid = (a * jax.nn.sigmoid(a)) * u * c_e
    acc_ref[...] += _dot(hid.astype(BF16), wd_ref[...])

    @pl.when(e == pl.num_programs(1) - 1)
    def _finish():
        out_ref[...] = _postnorm(x_ref[...], acc_ref[...], gate_ref[...], g_ref[...], b_ref[...])


def _router_pack(wg_r, bg_r, we_r, be_r):
    d = wg_r.shape[0]
    pad = LANES - N_GROUPS - N_EXPERTS
    rw = jnp.concatenate([wg_r, we_r, jnp.zeros((d, pad), F32)], axis=1).astype(BF16)
    rb = jnp.concatenate([bg_r, be_r, jnp.zeros((pad,), F32)]).reshape(1, LANES)
    return rw, rb


def moe_norm(h, rw, rb, wg, wu, wd, x, gate, ln_g, ln_b, tm):
    m, d = x.shape
    n_e, _, f = wg.shape
    return pl.pallas_call(
        _moe_kernel,
        grid=(m // tm, n_e),
        in_specs=[pl.BlockSpec((tm, d), lambda i, e: (i, 0)),
                  pl.BlockSpec(rw.shape, lambda i, e: (0, 0)),
                  pl.BlockSpec(rb.shape, lambda i, e: (0, 0)),
                  pl.BlockSpec((None, d, f), lambda i, e: (e, 0, 0)),
                  pl.BlockSpec((None, d, f), lambda i, e: (e, 0, 0)),
                  pl.BlockSpec((None, f, d), lambda i, e: (e, 0, 0)),
                  pl.BlockSpec((tm, d), lambda i, e: (i, 0)),
                  _seg_spec(gate, m, tm),
                  pl.BlockSpec((1, d), lambda i, e: (0, 0)),
                  pl.BlockSpec((1, d), lambda i, e: (0, 0))],
        out_specs=pl.BlockSpec((tm, d), lambda i, e: (i, 0)),
        out_shape=jax.ShapeDtypeStruct((m, d), F32),
        scratch_shapes=[pltpu.VMEM((tm, LANES), F32), pltpu.VMEM((tm, d), F32)],
        compiler_params=_params(2),
        name="moe_norm",
    )(h, rw, rb, wg, wu, wd, x, gate, ln_g.reshape(1, d), ln_b.reshape(1, d))


def _t5_bucket(dist):
    n = jnp.maximum(dist, 0)
    max_exact = REL_BUCKETS // 2
    nf = jnp.maximum(n, 1).astype(F32)
    log_b = max_exact + (jnp.log(nf / max_exact) / math.log(REL_MAX_DIST / max_exact) * (REL_BUCKETS - max_exact)).astype(jnp.int32)
    return jnp.where(n < max_exact, n, jnp.minimum(log_b, REL_BUCKETS - 1))


def _lookup(cols, bucket):
    out = jnp.zeros(jnp.broadcast_shapes(cols.shape[1:], bucket.shape), F32)
    for k in range(REL_BUCKETS):
        out = jnp.where(bucket == k, cols[k], out)
    return out


def _bias_of(tab, dist, mask):
    cols = tab.astype(F32).reshape(tab.shape + (1,) * dist.ndim)
    return jnp.where(mask, _lookup(cols, _t5_bucket(dist)), NEG)


def _toeplitz_tiles(tab, t):
    i = jnp.arange(t)[:, None]
    j = jnp.arange(t)[None, :]
    diag = _bias_of(tab, i - j, i >= j)
    left = _bias_of(tab, t + i - j, jnp.ones((t, t), bool))
    return jnp.stack([diag, left], axis=1)


def _far_bias(tab):
    return jnp.broadcast_to(tab[REL_BUCKETS - 1].astype(F32)[:, None, None], (tab.shape[1], 1, LANES))


def _attend_blocks(q, blocks, s_ref, m_floor=None):
    m_el = None
    for i, (load_k, _, logits_fn) in enumerate(blocks):
        s = logits_fn(_dot_nt(q, load_k()))
        s_ref[i] = s
        m_el = s if m_el is None else jnp.maximum(m_el, s)
    m = jnp.max(m_el, -1, keepdims=True)
    if m_floor is not None:
        m = jnp.maximum(m, m_floor)
    p_sum = acc = None
    for i, (_, load_v, _) in enumerate(blocks):
        p = jnp.exp(s_ref[i] - m)
        p_sum = p if p_sum is None else p_sum + p
        d = _dot(p.astype(BF16), load_v())
        acc = d if acc is None else acc + d
    return m, jnp.sum(p_sum, -1, keepdims=True), acc


def _per_query_tile(qt, n_tiles, body):
    for n in range(1, n_tiles + 1):
        pl.when(qt == n - 1)(functools.partial(body, n))


def _cast_once(qt, pairs):
    @pl.when(qt == 0)
    def _cast():
        for src, dst in pairs:
            dst[...] = src[...].astype(BF16)


def _causal_blocks(n, t, kb_ref, vb_ref, diag_fn, left_fn, far_fn):
    blocks = []
    for kt in range(n):
        fn = diag_fn if kt == n - 1 else (left_fn if kt == n - 2 else far_fn)
        blocks.append((lambda kt=kt: kb_ref[kt * t:(kt + 1) * t, :], lambda kt=kt: vb_ref[kt * t:(kt + 1) * t, :],
                       functools.partial(fn, kt)))
    return blocks


def _diff_prompt_kernel(q_ref, k_ref, v_ref, bias_ref, far_ref, lam_ref, sub_ref, o_ref, kb_ref, vb_ref, s_ref,
                        *, lam_init):
    t = q_ref.shape[0]
    qt = pl.program_id(2)
    _cast_once(qt, ((k_ref, kb_ref), (v_ref, vb_ref)))
    q = q_ref[...]
    lane = lax.broadcasted_iota(jnp.int32, q.shape, 1)
    q1 = jnp.where(lane < DIFF_DK, q, 0.0).astype(BF16)
    q2 = jnp.where(lane >= DIFF_DK, q, 0.0).astype(BF16)
    scale = DIFF_DK ** -0.5
    far = far_ref[...][:, :1]
    lam = lam_ref[...]
    lam_full = (jnp.exp(jnp.sum(lam[0:1] * lam[1:2], keepdims=True))
                - jnp.exp(jnp.sum(lam[2:3] * lam[3:4], keepdims=True)) + lam_init)

    def body(n):
        blocks = _causal_blocks(n, t, kb_ref, vb_ref,
                                lambda kt, s: s * scale + bias_ref[0],
                                lambda kt, s: s * scale + bias_ref[1],
                                lambda kt, s: s * scale + far)
        _, l1, a1 = _attend_blocks(q1, blocks, s_ref.at[0])
        _, l2, a2 = _attend_blocks(q2, blocks, s_ref.at[1])
        o = a1 / l1 - lam_full * (a2 / l2)
        o = o * lax.rsqrt(jnp.mean(o * o, -1, keepdims=True) + RMS_EPS) * sub_ref[...] * (1.0 - lam_init)
        o_ref[...] = o.astype(o_ref.dtype)

    _per_query_tile(qt, k_ref.shape[0] // t, body)


def diff_prompt(q, k, v, tiles, far, lam, subln, lam_init):
    b, s, _ = q.shape
    t = ATT_TILE
    kern = functools.partial(_diff_prompt_kernel, lam_init=lam_init)
    return pl.pallas_call(
        kern,
        grid=(b, DIFF_HEADS, s // t),
        in_specs=[pl.BlockSpec((None, t, HEAD_DIM), lambda bi, h, qt: (bi, qt, h)),
                  pl.BlockSpec((None, s, HEAD_DIM), lambda bi, h, qt: (bi, 0, h)),
                  pl.BlockSpec((None, s, HEAD_DIM), lambda bi, h, qt: (bi, 0, h)),
                  pl.BlockSpec((None, 2, t, t), lambda bi, h, qt: (h, 0, 0, 0)),
                  pl.BlockSpec((None, 1, LANES), lambda bi, h, qt: (h, 0, 0)),
                  pl.BlockSpec(lam.shape, lambda bi, h, qt: (0, 0)),
                  pl.BlockSpec((1, HEAD_DIM), lambda bi, h, qt: (0, 0))],
        out_specs=pl.BlockSpec((None, t, HEAD_DIM), lambda bi, h, qt: (bi, qt, h)),
        out_shape=jax.ShapeDtypeStruct((b, s, DIFF_HEADS * HEAD_DIM), BF16),
        scratch_shapes=[pltpu.VMEM((s, HEAD_DIM), BF16), pltpu.VMEM((s, HEAD_DIM), BF16),
                        pltpu.VMEM((2, s // t, t, t), F32)],
        compiler_params=_params(3),
        name="diff_prompt",
    )(q, k, v, tiles, far, lam, subln.reshape(1, HEAD_DIM))


def _cmp_ab_kernel(x_ref, pe_ref, w1_ref, o_ref, *, paged):
    half = w1_ref.shape[0] // 2
    hid = w1_ref.shape[1]
    per = PAGE_SIZE // NSA_CMP_STRIDE
    for g in range(NSA_KV_HEADS):
        if paged:
            n_rows = x_ref.shape[0] * per
            xg = jnp.concatenate(
                [x_ref[:, pl.ds(NSA_KV_HEADS * p + g, per, stride=NSA_KV_HEADS * NSA_CMP_STRIDE), :].reshape(n_rows, HEAD_DIM)
                 for p in range(NSA_CMP_STRIDE)], axis=1)
        else:
            xg = jnp.concatenate([x_ref[:, (NSA_KV_HEADS * p + g) * HEAD_DIM:(NSA_KV_HEADS * p + g + 1) * HEAD_DIM]
                                  for p in range(NSA_CMP_STRIDE)], axis=1)
        top = (xg + pe_ref[0:1, :]).astype(BF16)
        bot = (xg + pe_ref[1:2, :]).astype(BF16)
        o_ref[:, 2 * g * hid:(2 * g + 1) * hid] = _dot(top, w1_ref[:half, :])
        o_ref[:, (2 * g + 1) * hid:(2 * g + 2) * hid] = _dot(bot, w1_ref[half:, :])


def _cmp_fin_kernel(*refs, n_prefetch, nseq):
    ab_refs, w2_ref, o_ref = refs[n_prefetch:-2], refs[-2], refs[-1]
    per_seq = len(ab_refs) // nseq
    hid = w2_ref.shape[0]
    for s in range(nseq):
        mine = ab_refs[s * per_seq:(s + 1) * per_seq]
        ab = jnp.concatenate([r[...] for r in mine], axis=0) if per_seq > 1 else mine[0][...]
        n = ab.shape[0]
        row = lax.broadcasted_iota(jnp.int32, (n, HEAD_DIM), 0)
        for g in range(NSA_KV_HEADS):
            a = ab[:, 2 * g * hid:(2 * g + 1) * hid]
            b = ab[:, (2 * g + 1) * hid:(2 * g + 2) * hid]
            hidden = jax.nn.gelu(a + pltpu.roll(b, n - 1, 0), approximate=True)
            out = _dot(hidden.astype(BF16), w2_ref[...])
            o_ref[s, g] = jnp.where(row < n - 1, out, 0.0)


def nsa_compress(kseq, pe, w1, w2, page_table):
    rows = kseq.shape[0] * kseq.shape[1] // NSA_CMP_STRIDE
    width = NSA_CMP_STRIDE * NSA_KV_HEADS * HEAD_DIM
    hid = w1.shape[1]
    tr = math.gcd(rows, 512)
    per = PAGE_SIZE // NSA_CMP_STRIDE
    if page_table is None:
        x = kseq.reshape(rows, width)
        x_spec = pl.BlockSpec((tr, width), lambda i: (i, 0))
    else:
        x = kseq.reshape(kseq.shape[0], PAGE_SIZE * NSA_KV_HEADS, HEAD_DIM)
        x_spec = pl.BlockSpec((tr // per, PAGE_SIZE * NSA_KV_HEADS, HEAD_DIM), lambda i: (i, 0, 0))
    ab = pl.pallas_call(
        functools.partial(_cmp_ab_kernel, paged=page_table is not None),
        grid=(rows // tr,),
        in_specs=[x_spec,
                  pl.BlockSpec((2, width // 2), lambda i: (0, 0)),
                  pl.BlockSpec(w1.shape, lambda i: (0, 0))],
        out_specs=pl.BlockSpec((tr, 4 * hid), lambda i: (i, 0)),
        out_shape=jax.ShapeDtypeStruct((rows, 4 * hid), F32),
        compiler_params=_params(1),
        name="nsa_cmp_layer1",
    )(x, pe.reshape(2, width // 2), w1)
    n_cmp = 128
    if page_table is None:
        nb = kseq.shape[0]
        out_block = pl.BlockSpec((1, NSA_KV_HEADS, n_cmp, HEAD_DIM), lambda b: (b, 0, 0, 0))
        return pl.pallas_call(
            functools.partial(_cmp_fin_kernel, n_prefetch=0, nseq=1),
            grid=(nb,),
            in_specs=[pl.BlockSpec((n_cmp, 4 * hid), lambda b: (b, 0)), pl.BlockSpec(w2.shape, lambda b: (0, 0))],
            out_specs=out_block,
            out_shape=jax.ShapeDtypeStruct((nb, NSA_KV_HEADS, n_cmp, HEAD_DIM), F32),
            compiler_params=_params(1),
            name="nsa_cmp_layer2",
        )(ab, w2)
    nb, n_pages = page_table.shape
    per = PAGE_SIZE // NSA_CMP_STRIDE
    ab3 = ab.reshape(rows // per, per, 4 * hid)
    nseq = math.gcd(nb, 4)
    grid_spec = pltpu.PrefetchScalarGridSpec(
        num_scalar_prefetch=1,
        grid=(nb // nseq,),
        in_specs=[pl.BlockSpec((None, per, 4 * hid), lambda b, pt, s=s, p=p: (pt[b * nseq + s, p], 0, 0))
                  for s in range(nseq) for p in range(n_pages)]
        + [pl.BlockSpec(w2.shape, lambda b, pt: (0, 0))],
        out_specs=pl.BlockSpec((nseq, NSA_KV_HEADS, n_cmp, HEAD_DIM), lambda b, pt: (b, 0, 0, 0)),
    )
    return pl.pallas_call(
        functools.partial(_cmp_fin_kernel, n_prefetch=1, nseq=nseq),
        grid_spec=grid_spec,
        out_shape=jax.ShapeDtypeStruct((nb, NSA_KV_HEADS, n_cmp, HEAD_DIM), F32),
        compiler_params=_params(1),
        name="nsa_cmp_layer2_paged",
    )(page_table, *([ab3] * (nseq * n_pages)), w2)


def _rank_select(score, lane, n_valid_lanes, topk):
    rank = jnp.zeros(score.shape, jnp.int32)
    for j in range(n_valid_lanes):
        col = score[:, j:j + 1]
        beats = (col > score) | ((col == score) & (j < lane))
        rank = rank + beats.astype(jnp.int32)
    return rank < topk


def _nsa_cmp_kernel(q_ref, kc_ref, vc_ref, bias_ref, cov_ref, o_ref, sel_ref, *, qpos0, n_slc):
    t = q_ref.shape[1]
    qt = pl.program_id(2)
    scale = HEAD_DIM ** -0.5
    lane = lax.broadcasted_iota(jnp.int32, (t, LANES), 1)
    qpos = qpos0 + qt * t + lax.broadcasted_iota(jnp.int32, (t, LANES), 0)
    cur = qpos // NSA_SEL_BLOCK
    valid = (lane * NSA_SEL_BLOCK <= qpos) & (lane < n_slc)
    forced = valid & ((lane == 0) | (lane == cur) | (lane == cur - 1))
    for i in range(q_ref.shape[0]):
        kc = kc_ref[i].astype(BF16)
        vc = vc_ref[i].astype(BF16)
        imp = jnp.zeros((t, LANES), F32)
        for r in range(NSA_GROUP):
            qr = q_ref[i, :, r * HEAD_DIM:(r + 1) * HEAD_DIM].astype(BF16)
            bias = bias_ref[r]
            live = bias > 0.5 * NEG
            s = _dot_nt(qr, kc) * scale + bias
            m = jnp.max(s, -1, keepdims=True)
            p = jnp.where(live, jnp.exp(s - m), 0.0)
            p = p / jnp.maximum(jnp.sum(p, -1, keepdims=True), jnp.finfo(F32).tiny)
            pb = p.astype(BF16)
            o_ref[i, :, r * HEAD_DIM:(r + 1) * HEAD_DIM] = _dot(pb, vc)
            imp = imp + _dot(pb, cov_ref[...])
        score = jnp.where(forced, jnp.inf, jnp.where(valid, imp, -jnp.inf))
        chosen = _rank_select(score, lane, n_slc, min(NSA_SEL_TOPK, n_slc))
        sel_ref[i] = (valid & chosen).astype(F32)


def nsa_cmp_select(q_arr, q_col0, t, kcmp, vcmp, tab_n, qpos0, seq_len):
    b, sq, _ = q_arr.shape
    n_cmp = (seq_len - NSA_CMP_LEN) // NSA_CMP_STRIDE + 1
    n_slc = -(-seq_len // NSA_SEL_BLOCK)
    i = jnp.arange(LANES)
    qpos = qpos0 + jnp.arange(sq)
    dist = qpos[:, None] - (i * NSA_CMP_STRIDE + NSA_CMP_LEN - 1)[None, :]
    bias = _bias_of(tab_n, dist, (dist >= 0) & (i < n_cmp)[None, :]).reshape(NSA_KV_HEADS, NSA_GROUP, sq, LANES)
    cs = np.arange(LANES)[:, None] * NSA_CMP_STRIDE
    ss = np.arange(LANES)[None, :] * NSA_SEL_BLOCK
    covers = ((cs < ss + NSA_SEL_BLOCK) & (cs + NSA_CMP_LEN > ss)
              & (np.arange(LANES)[:, None] < n_cmp) & (np.arange(LANES)[None, :] < n_slc))
    covers = jnp.asarray(covers, BF16)
    kern = functools.partial(_nsa_cmp_kernel, qpos0=qpos0, n_slc=n_slc)
    gw = NSA_GROUP * HEAD_DIM
    nseq = math.gcd(b, 8) if sq == t else 1
    return pl.pallas_call(
        kern,
        grid=(b // nseq, NSA_KV_HEADS, sq // t),
        in_specs=[pl.BlockSpec((nseq, t, gw), lambda bi, g, qt: (bi, qt, q_col0 + g)),
                  pl.BlockSpec((nseq, None, LANES, HEAD_DIM), lambda bi, g, qt: (bi, g, 0, 0)),
                  pl.BlockSpec((nseq, None, LANES, HEAD_DIM), lambda bi, g, qt: (bi, g, 0, 0)),
                  pl.BlockSpec((None, NSA_GROUP, t, LANES), lambda bi, g, qt: (g, 0, qt, 0)),
                  pl.BlockSpec((LANES, LANES), lambda bi, g, qt: (0, 0))],
        out_specs=[pl.BlockSpec((nseq, t, gw), lambda bi, g, qt: (bi, qt, g)),
                   pl.BlockSpec((nseq, None, t, LANES), lambda bi, g, qt: (bi, g, qt, 0))],
        out_shape=[jax.ShapeDtypeStruct((b, sq, NSA_KV_HEADS * gw), F32),
                   jax.ShapeDtypeStruct((b, NSA_KV_HEADS, sq, LANES), F32)],
        compiler_params=_params(3),
        name="nsa_cmp_select",
    )(q_arr, kcmp, vcmp, bias, covers)


def _gate_pad(g):
    lead = g.shape[:-1]
    g = g.reshape(*lead, NSA_KV_HEADS, NSA_GROUP * 3)
    g = jnp.pad(g, [(0, 0)] * (len(lead) + 1) + [(0, LANES - NSA_GROUP * 3)])
    return g.reshape(*lead, NSA_KV_HEADS * LANES)


def _nsa_prompt_kernel(q_ref, ks_ref, vs_ref, kw_ref, vw_ref, sel_ref, ocmp_ref, g_ref, bias_ref, far_ref, e_ref,
                       o_ref, ksb_ref, vsb_ref, kwb_ref, vwb_ref, s_ref, sw_ref):
    t = q_ref.shape[0]
    qt = pl.program_id(2)
    scale = HEAD_DIM ** -0.5
    _cast_once(qt, ((ks_ref, ksb_ref), (vs_ref, vsb_ref), (kw_ref, kwb_ref), (vw_ref, vwb_ref)))
    q4 = jnp.concatenate([q_ref[:, r * HEAD_DIM:(r + 1) * HEAD_DIM] for r in range(NSA_GROUP)], axis=0).astype(BF16)
    sel4 = jnp.concatenate([sel_ref[...].astype(BF16)] * NSA_GROUP, axis=0)
    far = far_ref[...][:, :1]
    rows = NSA_GROUP * t

    def chosen(kt):
        return _dot(sel4, e_ref[:, kt * t:(kt + 1) * t]) > 0.5

    def body(n):
        slc = _causal_blocks(n, t, ksb_ref, vsb_ref,
                             lambda kt, s: jnp.where(chosen(kt), s * scale + bias_ref[0], NEG),
                             lambda kt, s: jnp.where(chosen(kt), s * scale + bias_ref[1], NEG),
                             lambda kt, s: jnp.where(chosen(kt), s * scale + far, NEG))
        _, l_s, a_s = _attend_blocks(q4, slc, s_ref)

        def edge(kt, s):
            query = lax.broadcasted_iota(jnp.int32, (rows, t), 0) & (t - 1)
            key = lax.broadcasted_iota(jnp.int32, (rows, t), 1)
            return jnp.where(key > query, s * scale + far, NEG)

        win = _causal_blocks(n, t, kwb_ref, vwb_ref,
                             lambda kt, s: s * scale + bias_ref[0],
                             lambda kt, s: s * scale + bias_ref[1],
                             edge)[max(n - 3, 0):]
        _, l_w, a_w = _attend_blocks(q4, win, sw_ref)
        o_slc = a_s / l_s
        o_win = a_w / l_w
        gate = jax.nn.sigmoid(g_ref[...])
        for r in range(NSA_GROUP):
            o = (gate[:, 3 * r:3 * r + 1] * ocmp_ref[:, r * HEAD_DIM:(r + 1) * HEAD_DIM]
                 + gate[:, 3 * r + 1:3 * r + 2] * o_slc[r * t:(r + 1) * t]
                 + gate[:, 3 * r + 2:3 * r + 3] * o_win[r * t:(r + 1) * t])
            o_ref[:, r * HEAD_DIM:(r + 1) * HEAD_DIM] = o.astype(o_ref.dtype)

    _per_query_tile(qt, ks_ref.shape[0] // t, body)


def nsa_prompt(nq, ks, vs, kw, vw, sel, o_cmp, gates, tiles, far):
    b, s, _ = nq.shape
    t = ATT_TILE
    assert 2 * t == NSA_WINDOW
    nkt = s // t
    gw = NSA_GROUP * HEAD_DIM
    expand = jnp.asarray(np.arange(LANES)[:, None] == np.arange(s)[None, :] // NSA_SEL_BLOCK, BF16)
    rows = NSA_GROUP * t
    tiles4 = jnp.swapaxes(tiles.reshape(NSA_KV_HEADS, NSA_GROUP, 2, t, t), 1, 2).reshape(NSA_KV_HEADS, 2, rows, t)
    far4 = jnp.broadcast_to(far.reshape(NSA_KV_HEADS, NSA_GROUP, 1, LANES),
                            (NSA_KV_HEADS, NSA_GROUP, t, LANES)).reshape(NSA_KV_HEADS, rows, LANES)
    kv = pl.BlockSpec((None, s, HEAD_DIM), lambda bi, g, qt: (bi, 0, g))
    return pl.pallas_call(
        _nsa_prompt_kernel,
        grid=(b, NSA_KV_HEADS, nkt),
        in_specs=[pl.BlockSpec((None, t, gw), lambda bi, g, qt: (bi, qt, g)),
                  kv, kv, kv, kv,
                  pl.BlockSpec((None, None, t, LANES), lambda bi, g, qt: (bi, g, qt, 0)),
                  pl.BlockSpec((None, t, gw), lambda bi, g, qt: (bi, qt, g)),
                  pl.BlockSpec((None, t, LANES), lambda bi, g, qt: (bi, qt, g)),
                  pl.BlockSpec((None, 2, rows, t), lambda bi, g, qt: (g, 0, 0, 0)),
                  pl.BlockSpec((None, rows, LANES), lambda bi, g, qt: (g, 0, 0)),
                  pl.BlockSpec((LANES, s), lambda bi, g, qt: (0, 0))],
        out_specs=pl.BlockSpec((None, t, gw), lambda bi, g, qt: (bi, qt, g)),
        out_shape=jax.ShapeDtypeStruct((b, s, NSA_KV_HEADS * gw), BF16),
        scratch_shapes=[pltpu.VMEM((s, HEAD_DIM), BF16)] * 4
        + [pltpu.VMEM((nkt, rows, t), F32), pltpu.VMEM((min(nkt, 3), rows, t), F32)],
        compiler_params=_params(3),
        name="nsa_prompt",
    )(nq, ks, vs, kw, vw, sel, o_cmp, gates, tiles4, far4, expand)


def _moba_prompt_kernel(q_ref, k_ref, v_ref, bias_ref, far_ref, o_ref, kmean_ref, kb_ref, vb_ref, s_ref):
    t = q_ref.shape[0]
    qt = pl.program_id(2)
    nb = k_ref.shape[0] // t
    scale = HEAD_DIM ** -0.5
    _cast_once(qt, ((k_ref, kb_ref), (v_ref, vb_ref)))

    @pl.when(qt == 0)
    def _means():
        kmean_ref[...] = jnp.zeros_like(kmean_ref)
        for j in range(nb):
            kmean_ref[j:j + 1, :] = jnp.mean(k_ref[j * t:(j + 1) * t, :], axis=0, keepdims=True)

    q = q_ref[...].astype(BF16)
    lane = lax.broadcasted_iota(jnp.int32, (t, LANES), 1)
    score = jnp.where(lane < qt, _dot_nt(q, kmean_ref[...].astype(BF16)), -jnp.inf)
    sel = (lane < qt) & _rank_select(score, lane, nb, MOBA_TOPK)
    far = far_ref[...][:, :1]

    def body(n):
        blocks = _causal_blocks(n, t, kb_ref, vb_ref,
                                lambda kt, s: s * scale + bias_ref[0],
                                lambda kt, s: jnp.where(sel[:, kt:kt + 1], s * scale + bias_ref[1], NEG),
                                lambda kt, s: jnp.where(sel[:, kt:kt + 1], s * scale + far, NEG))
        _, l, acc = _attend_blocks(q, blocks, s_ref)
        o_ref[...] = (acc / l).astype(o_ref.dtype)

    _per_query_tile(qt, nb, body)


def moba_prompt(q, k, v, tiles, far):
    b, s, _ = q.shape
    t = ATT_TILE
    assert t == MOBA_BLOCK and s // t <= LANES
    return pl.pallas_call(
        _moba_prompt_kernel,
        grid=(b, MOBA_HEADS, s // t),
        in_specs=[pl.BlockSpec((None, t, HEAD_DIM), lambda bi, h, qt: (bi, qt, h)),
                  pl.BlockSpec((None, s, HEAD_DIM), lambda bi, h, qt: (bi, 0, h)),
                  pl.BlockSpec((None, s, HEAD_DIM), lambda bi, h, qt: (bi, 0, h)),
                  pl.BlockSpec((None, 2, t, t), lambda bi, h, qt: (h, 0, 0, 0)),
                  pl.BlockSpec((None, 1, LANES), lambda bi, h, qt: (h, 0, 0))],
        out_specs=pl.BlockSpec((None, t, HEAD_DIM), lambda bi, h, qt: (bi, qt, h)),
        out_shape=jax.ShapeDtypeStruct((b, s, MOBA_HEADS * HEAD_DIM), BF16),
        scratch_shapes=[pltpu.VMEM((LANES, HEAD_DIM), F32), pltpu.VMEM((s, HEAD_DIM), BF16),
                        pltpu.VMEM((s, HEAD_DIM), BF16), pltpu.VMEM((s // t, t, t), F32)],
        compiler_params=_params(3),
        name="moba_prompt",
    )(q, k, v, tiles, far)


def _decode_bias(tab, row_t, row_col, row_kv, n_kv, qpos0, kpos, window=None):
    c = kpos.shape[0] * n_kv
    col_kv = np.arange(c) % n_kv
    kp = np.repeat(kpos, n_kv)
    dist = (qpos0 + row_t)[:, None] - kp[None, :]
    ok = (row_kv[:, None] == col_kv[None, :]) & (dist >= 0) & (kp >= 0)[None, :]
    if window is not None:
        ok = ok & (dist < window)
    per_row = jnp.take(tab.astype(F32), jnp.asarray(row_col, jnp.int32), axis=1)
    b = _lookup(per_row[:, :, None], _t5_bucket(jnp.asarray(dist)))
    return jnp.where(jnp.asarray(ok), b, NEG)


def _decode_kernel(*refs, scale, ppc, n_pages, masked, diff_lam_init):
    pt_ref, q_ref, kn_ref, vn_ref, bnew_ref, tiles_ref = refs[:6]
    pos = 6
    rm_ref = None
    if masked:
        rm_ref = refs[pos]
        pos += 1
    lam_ref = sub_ref = None
    if diff_lam_init is not None:
        lam_ref, sub_ref = refs[pos:pos + 2]
        pos += 2
    k_refs = refs[pos:pos + ppc]
    v_refs = refs[pos + ppc:pos + 2 * ppc]
    o_ref, m_ref, l_ref, acc_ref, s_ref = refs[pos + 2 * ppc:]
    single = ppc == n_pages
    c = 0 if single else pl.program_id(1)
    q = q_ref[...].astype(BF16)
    rows = q.shape[0]
    cols = k_refs[0].shape[0]

    @pl.when(pl.program_id(1) == 0)
    def _init():
        m_ref[...] = jnp.full_like(m_ref, NEG)
        l_ref[...] = jnp.zeros_like(l_ref)
        acc_ref[...] = jnp.zeros_like(acc_ref)

    s_new = _dot_nt(q, kn_ref[...].astype(BF16)) * scale + bnew_ref[...]
    if not single:
        s_new = jnp.where(c == 0, s_new, NEG)

    def page_logits(p, s):
        gp = c * ppc + p
        if single:
            tile = tiles_ref[0 if gp == 0 else (2 if gp == n_pages - 1 else 1)]
        else:
            tile = tiles_ref[1]
            if p == 0:
                tile = jnp.where(gp == 0, tiles_ref[0], tile)
            if p == ppc - 1:
                tile = jnp.where(gp == n_pages - 1, tiles_ref[2], tile)
        s = s * scale + tile
        if masked:
            assert single
            col = lax.broadcasted_iota(jnp.int32, (rows, cols), 1)
            keep = jnp.where(col < cols // 2, rm_ref[:, 2 * gp:2 * gp + 1], rm_ref[:, 2 * gp + 1:2 * gp + 2])
            s = jnp.where(keep > 0.5, s, NEG)
        return s

    blocks = [(lambda p=p: k_refs[p][...].astype(BF16), lambda p=p: v_refs[p][...].astype(BF16),
               functools.partial(page_logits, p)) for p in range(ppc)]
    m_old = m_ref[...]
    m, l_pages, acc_pages = _attend_blocks(q, blocks, s_ref,
                                           m_floor=jnp.maximum(m_old, jnp.max(s_new, -1, keepdims=True)))
    alpha = jnp.exp(m_old - m)
    p_new = jnp.exp(s_new - m)
    l_ref[...] = alpha * l_ref[...] + l_pages + jnp.sum(p_new, -1, keepdims=True)
    acc_ref[...] = alpha * acc_ref[...] + acc_pages + _dot(p_new.astype(BF16), vn_ref[...].astype(BF16))
    m_ref[...] = m

    @pl.when(pl.program_id(1) == pl.num_programs(1) - 1)
    def _finish():
        o = acc_ref[...] / l_ref[...]
        if diff_lam_init is not None:
            lam = lam_ref[...]
            lam_full = (jnp.exp(jnp.sum(lam[0:1] * lam[1:2], keepdims=True))
                        - jnp.exp(jnp.sum(lam[2:3] * lam[3:4], keepdims=True)) + diff_lam_init)
            half = rows // 2
            o = o[:half] - lam_full * o[half:]
            o = o * lax.rsqrt(jnp.mean(o * o, -1, keepdims=True) + RMS_EPS) * sub_ref[...] * (1.0 - diff_lam_init)
        o_ref[...] = o.astype(o_ref.dtype)


def decode_attn(page_table, q, k_new, v_new, bias_new, tiles, k_pages, v_pages, n_chunks, scale, out_dtype,
                row_mask=None, diff=None, name="decode_attn"):
    bs, r, _ = q.shape
    n_pages = page_table.shape[1]
    ppc = n_pages // n_chunks
    c = k_pages.shape[1]
    per_b = lambda shape: pl.BlockSpec((None,) + shape, lambda b, ch, pt: (b, 0, 0))
    const = lambda shape: pl.BlockSpec(shape, lambda b, ch, pt: (0,) * len(shape))
    in_specs = [per_b((r, HEAD_DIM)), per_b((LANES, HEAD_DIM)), per_b((LANES, HEAD_DIM)),
                const((r, LANES)), const((3, r, c))]
    args = [q, k_new, v_new, bias_new, tiles]
    if row_mask is not None:
        in_specs.append(per_b((r, LANES)))
        args.append(row_mask)
    r_out = r
    if diff is not None:
        in_specs += [const(diff[0].shape), const((1, HEAD_DIM))]
        args += [diff[0], diff[1].reshape(1, HEAD_DIM)]
        r_out = r // 2
    page = lambda p: pl.BlockSpec((None, c, HEAD_DIM), lambda b, ch, pt: (pt[b, ch * ppc + p], 0, 0))
    in_specs += [page(p) for p in range(ppc)] * 2
    args += [k_pages] * ppc + [v_pages] * ppc
    kern = functools.partial(_decode_kernel, scale=scale, ppc=ppc, n_pages=n_pages, masked=row_mask is not None,
                             diff_lam_init=None if diff is None else diff[2])
    grid_spec = pltpu.PrefetchScalarGridSpec(
        num_scalar_prefetch=1,
        grid=(bs, n_chunks),
        in_specs=in_specs,
        out_specs=pl.BlockSpec((None, r_out, HEAD_DIM), lambda b, ch, pt: (b, 0, 0)),
        scratch_shapes=[pltpu.VMEM((r, 1), F32), pltpu.VMEM((r, 1), F32), pltpu.VMEM((r, HEAD_DIM), F32),
                        pltpu.VMEM((ppc, r, c), F32)],
    )
    return pl.pallas_call(
        kern,
        grid_spec=grid_spec,
        out_shape=jax.ShapeDtypeStruct((bs, r_out, HEAD_DIM), out_dtype),
        compiler_params=_params(2),
        name=name,
    )(page_table, *args)


def _pad_rows(x, n):
    return jnp.pad(x, ((0, 0), (0, n - x.shape[1]), (0, 0)))


def _moba_decode_kernel(*refs, bps):
    pt_ref, q_ref, kn_ref, vn_ref, bnew_ref, tiles_ref = refs[:6]
    k_refs = refs[6:6 + 2 * bps]
    v_refs = refs[6 + 2 * bps:6 + 4 * bps]
    o_ref, ssel_ref, mb_ref, lb_ref, ab_ref, s_ref = refs[6 + 4 * bps:]
    j = pl.program_id(1)
    nb = pl.num_programs(1)
    n_blk = mb_ref.shape[0]
    q = q_ref[...].astype(BF16)
    rows = q.shape[0]
    scale = HEAD_DIM ** -0.5
    lane = lax.broadcasted_iota(jnp.int32, (rows, LANES), 1)

    @pl.when(j == 0)
    def _init():
        ssel_ref[...] = jnp.full_like(ssel_ref, -jnp.inf)

    for i in range(bps):
        blk = j * bps + i
        k0_ref, k1_ref, v0_ref, v1_ref = k_refs[2 * i], k_refs[2 * i + 1], v_refs[2 * i], v_refs[2 * i + 1]
        ksum = (jnp.sum(k0_ref[...].reshape(PAGE_SIZE, MOBA_HEADS, HEAD_DIM), axis=0)
                + jnp.sum(k1_ref[...].reshape(PAGE_SIZE, MOBA_HEADS, HEAD_DIM), axis=0))
        last = tiles_ref[0]
        if i == bps - 1:
            last = jnp.where(j == nb - 1, tiles_ref[1], last)
        blocks = [(lambda r=k0_ref: r[...].astype(BF16), lambda r=v0_ref: r[...].astype(BF16),
                   lambda s: s * scale + tiles_ref[0]),
                  (lambda r=k1_ref: r[...].astype(BF16), lambda r=v1_ref: r[...].astype(BF16),
                   lambda s, last=last: s * scale + last)]
        m_blk, l_blk, acc_blk = _attend_blocks(q, blocks, s_ref.at[i])
        mb_ref[blk] = m_blk
        lb_ref[blk] = l_blk
        ab_ref[blk] = acc_blk
        kmean = (ksum * (1.0 / MOBA_BLOCK)).astype(BF16).astype(F32)
        kmean_rows = jnp.concatenate([kmean] * (rows // MOBA_HEADS), axis=0)
        score = jnp.sum(q.astype(F32) * kmean_rows, -1, keepdims=True)
        ssel_ref[...] = jnp.where(lane == blk, score, ssel_ref[...])

    @pl.when(j == nb - 1)
    def _finish():
        chosen = (lane < n_blk) & _rank_select(ssel_ref[...], lane, n_blk, MOBA_TOPK)
        sel = chosen.astype(F32)
        s_new = _dot_nt(q, kn_ref[...].astype(BF16)) * scale + bnew_ref[...]
        keep = [jnp.sum(jnp.where(lane == b, sel, 0.0), -1, keepdims=True) > 0.5 for b in range(n_blk)]
        m_tot = jnp.max(s_new, -1, keepdims=True)
        for b in range(n_blk):
            m_tot = jnp.maximum(m_tot, jnp.where(keep[b], mb_ref[b], NEG))
        p_new = jnp.exp(s_new - m_tot)
        l_tot = jnp.sum(p_new, -1, keepdims=True)
        acc = _dot(p_new.astype(BF16), vn_ref[...].astype(BF16))
        for b in range(n_blk):
            w = jnp.where(keep[b], jnp.exp(mb_ref[b] - m_tot), 0.0)
            l_tot = l_tot + w * lb_ref[b]
            acc = acc + w * ab_ref[b]
        o_ref[...] = (acc / l_tot).astype(o_ref.dtype)


def moba_decode(page_table, q, k_new, v_new, bias_new, tiles, k_pages, v_pages):
    bs, r, _ = q.shape
    n_pages = page_table.shape[1]
    ppb = MOBA_BLOCK // PAGE_SIZE
    assert ppb == 2 and n_pages % ppb == 0 and n_pages // ppb >= MOBA_TOPK
    n_blk = n_pages // ppb
    bps = math.gcd(n_blk, 4)
    c = k_pages.shape[1]
    per_b = lambda shape: pl.BlockSpec((None,) + shape, lambda b, j, pt: (b, 0, 0))
    const = lambda shape: pl.BlockSpec(shape, lambda b, j, pt: (0,) * len(shape))
    page = lambda p: pl.BlockSpec((None, c, HEAD_DIM), lambda b, j, pt: (pt[b, ppb * bps * j + p], 0, 0))
    pages = [page(p) for p in range(ppb * bps)]
    grid_spec = pltpu.PrefetchScalarGridSpec(
        num_scalar_prefetch=1,
        grid=(bs, n_blk // bps),
        in_specs=[per_b((r, HEAD_DIM)), per_b((LANES, HEAD_DIM)), per_b((LANES, HEAD_DIM)),
                  const((r, LANES)), const((2, r, c))] + pages + pages,
        out_specs=pl.BlockSpec((None, r, HEAD_DIM), lambda b, j, pt: (b, 0, 0)),
        scratch_shapes=[pltpu.VMEM((r, LANES), F32), pltpu.VMEM((n_blk, r, 1), F32), pltpu.VMEM((n_blk, r, 1), F32),
                        pltpu.VMEM((n_blk, r, HEAD_DIM), F32), pltpu.VMEM((bps, ppb, r, c), F32)],
    )
    return pl.pallas_call(
        functools.partial(_moba_decode_kernel, bps=bps),
        grid_spec=grid_spec,
        out_shape=jax.ShapeDtypeStruct((bs, r, HEAD_DIM), BF16),
        compiler_params=_params(2),
        name="moba_decode",
    )(page_table, q, k_new, v_new, bias_new, tiles, *([k_pages] * (ppb * bps)), *([v_pages] * (ppb * bps)))


def _nsa_combine_kernel(oc_ref, os_ref, ow_ref, g_ref, o_ref):
    gate = jax.nn.sigmoid(g_ref[...])
    o = gate[..., 0:1] * oc_ref[...] + gate[..., 1:2] * os_ref[...] + gate[..., 2:3] * ow_ref[...]
    o_ref[...] = o.astype(o_ref.dtype)


def nsa_combine(o_cmp, o_slc, o_win, gates):
    full = lambda a: pl.BlockSpec(a.shape, lambda: (0,) * a.ndim)
    return pl.pallas_call(
        _nsa_combine_kernel,
        in_specs=[full(o_cmp), full(o_slc), full(o_win), full(gates)],
        out_specs=full(o_cmp),
        out_shape=jax.ShapeDtypeStruct(o_cmp.shape, BF16),
        compiler_params=pltpu.CompilerParams(vmem_limit_bytes=VMEM_LIMIT),
        name="nsa_combine",
    )(o_cmp, o_slc, o_win, gates)


AB_MAIN = 2 * DIFF_HEADS * 2 * DIFF_DK + DIFF_HEADS * HEAD_DIM + NSA_HEADS * HEAD_DIM + 6 * NSA_KV_HEADS * HEAD_DIM
ROW_TILE = 512


def _in_proj(h, w, tn):
    m = h.shape[0]
    n = w.shape[1]
    n_main = n // tn * tn
    tm = min(m, 2 * ROW_TILE)
    main = matmul(h, w, n_main, tm, tn, "in_proj")
    if n_main == n:
        return main, None
    tail = jnp.pad(w[:, n_main:], ((0, 0), (0, LANES - (n - n_main))))
    return main, matmul(h, tail, LANES, tm, LANES, "in_proj_tail")[:, :n - n_main]


def _in_proj_split(h, w, widths, lead):
    m = h.shape[0]
    tm = min(m, 2 * ROW_TILE)
    outs, col = [], 0
    for width in widths:
        tn = math.gcd(width, 1024)
        outs.append(matmul(h, w, width, tm, tn, "in_proj", col0=col).reshape(*lead, width))
        col += width
    return outs


def _even_prompt(hp, w_in, b, s, lam, subln, lam_init, pe, w1, w2, tab, tiles, far):
    hd = DIFF_HEADS * HEAD_DIM
    kv = NSA_KV_HEADS * HEAD_DIM
    dq, dk, dv, nq, kc, vc, ks, vs, kw, vw = _in_proj_split(hp, w_in, (hd,) * 4 + (kv,) * 6, (b, s))
    tail = jnp.pad(w_in[:, AB_MAIN:], ((0, 0), (0, LANES - (w_in.shape[1] - AB_MAIN))))
    gates = matmul(hp, tail, LANES, min(b * s, 2 * ROW_TILE), LANES, "in_proj_tail")[:, :w_in.shape[1] - AB_MAIN]
    o_diff = diff_prompt(dq, dk, dv, tiles[:DIFF_HEADS], far[:DIFF_HEADS], lam, subln, lam_init)
    tab_n = tab[:, DIFF_HEADS:]
    kcmp = nsa_compress(kc, pe[0], w1[0], w2[0], None)
    vcmp = nsa_compress(vc, pe[1], w1[1], w2[1], None)
    o_cmp, sel = nsa_cmp_select(nq, 0, ATT_TILE, kcmp, vcmp, tab_n, 0, s)
    o_nsa = nsa_prompt(nq, ks, vs, kw, vw, sel, o_cmp, _gate_pad(gates).reshape(b, s, NSA_KV_HEADS * LANES),
                       tiles[DIFF_HEADS:], far[DIFF_HEADS:])
    keep = min(NSA_WINDOW, s)
    state = (dk.reshape(b, s, DIFF_HEADS, HEAD_DIM), dv.reshape(b, s, DIFF_HEADS, HEAD_DIM),
             *[a.reshape(b, s, NSA_KV_HEADS, HEAD_DIM) for a in (kc, vc, ks, vs)],
             kw[:, s - keep:].reshape(b, keep, NSA_KV_HEADS, HEAD_DIM),
             vw[:, s - keep:].reshape(b, keep, NSA_KV_HEADS, HEAD_DIM))
    return o_diff.reshape(b * s, hd), o_nsa.reshape(b * s, NSA_HEADS * HEAD_DIM), state


def _even_sample(proj, gates, bs, ts, page_table, cache_dk, cache_dv, cache_ck, cache_cv, cache_sk, cache_sv,
                 win_k, win_v, lam, subln, lam_init, pe, w1, w2, tab):
    n_pool = cache_dk.shape[0]
    n_pages = page_table.shape[1]
    past = n_pages * PAGE_SIZE
    hd = DIFF_HEADS * HEAD_DIM
    kv = NSA_KV_HEADS * HEAD_DIM
    p3 = proj.reshape(bs, ts, AB_MAIN)
    dq, dk, dv, nq = [p3[..., i * hd:(i + 1) * hd] for i in range(4)]
    c0 = AB_MAIN - 6 * kv
    kc, vc, ks, vs, kw, vw = [p3[..., c0 + i * kv:c0 + (i + 1) * kv] for i in range(6)]
    tab_d, tab_n = tab[:, :DIFF_HEADS], tab[:, DIFF_HEADS:]
    new_pos = np.where(np.arange(LANES) < ts, past + np.arange(LANES), -1)

    r = ts * DIFF_HEADS
    q = dq.reshape(bs, r, HEAD_DIM)
    lane = jnp.arange(HEAD_DIM)
    q2 = jnp.concatenate([jnp.where(lane < DIFF_DK, q, 0.0), jnp.where(lane >= DIFF_DK, q, 0.0)], axis=1)
    row_t = np.tile(np.repeat(np.arange(ts), DIFF_HEADS), 2)
    row_h = np.tile(np.arange(DIFF_HEADS), 2 * ts)
    page_pos = lambda p: p * PAGE_SIZE + np.arange(PAGE_SIZE)
    d_bias = lambda kpos: _decode_bias(tab_d, row_t, row_h, row_h, DIFF_HEADS, past, kpos)
    tiles_d = jnp.stack([d_bias(page_pos(p)) for p in (0, 1, n_pages - 1)])
    o_diff = decode_attn(page_table, q2, _pad_rows(dk.reshape(bs, r, HEAD_DIM), LANES),
                         _pad_rows(dv.reshape(bs, r, HEAD_DIM), LANES), d_bias(new_pos[:LANES // DIFF_HEADS]), tiles_d,
                         cache_dk.reshape(n_pool, PAGE_SIZE * DIFF_HEADS, HEAD_DIM),
                         cache_dv.reshape(n_pool, PAGE_SIZE * DIFF_HEADS, HEAD_DIM),
                         1, DIFF_DK ** -0.5, BF16, diff=(lam, subln, lam_init), name="diff_decode")

    rn = ts * NSA_HEADS
    qn = nq.reshape(bs, rn, HEAD_DIM)
    row_tn = np.repeat(np.arange(ts), NSA_HEADS)
    row_cn = np.tile(np.arange(NSA_HEADS), ts)
    row_gn = row_cn // NSA_GROUP
    kcmp = nsa_compress(cache_ck, pe[0], w1[0], w2[0], page_table)
    vcmp = nsa_compress(cache_cv, pe[1], w1[1], w2[1], page_table)
    o_cmp, sel = nsa_cmp_select(p3, 6, ts, kcmp, vcmp, tab_n, past, past + ts)
    row_mask = jnp.broadcast_to(jnp.swapaxes(sel, 1, 2)[:, :, :, None, :], (bs, ts, NSA_KV_HEADS, NSA_GROUP, LANES))
    n_bias = lambda kpos, window=None: _decode_bias(tab_n, row_tn, row_cn, row_gn, NSA_KV_HEADS, past, kpos, window)
    new_n = new_pos[:LANES // NSA_KV_HEADS]
    k_new = lambda a: _pad_rows(a.reshape(bs, ts * NSA_KV_HEADS, HEAD_DIM), LANES)
    tiles_s = jnp.stack([n_bias(page_pos(p)) for p in (0, 1, n_pages - 1)])
    o_slc = decode_attn(page_table, qn, k_new(ks), k_new(vs), n_bias(new_n), tiles_s,
                        cache_sk.reshape(n_pool, PAGE_SIZE * NSA_KV_HEADS, HEAD_DIM),
                        cache_sv.reshape(n_pool, PAGE_SIZE * NSA_KV_HEADS, HEAD_DIM),
                        1, HEAD_DIM ** -0.5, F32, row_mask=row_mask.reshape(bs, rn, LANES), name="nsa_slc_decode")
    w_buf = win_k.shape[1]
    w_pages = w_buf // PAGE_SIZE
    win_table = jnp.arange(bs * w_pages, dtype=jnp.int32).reshape(bs, w_pages)
    win_pos = lambda p: past - w_buf + p * PAGE_SIZE + np.arange(PAGE_SIZE)
    tiles_w = jnp.stack([n_bias(win_pos(p), NSA_WINDOW) for p in (0, 1, w_pages - 1)])
    o_win = decode_attn(win_table, qn, k_new(kw), k_new(vw), n_bias(new_n, NSA_WINDOW), tiles_w,
                        win_k.reshape(bs * w_pages, PAGE_SIZE * NSA_KV_HEADS, HEAD_DIM),
                        win_v.reshape(bs * w_pages, PAGE_SIZE * NSA_KV_HEADS, HEAD_DIM),
                        1, HEAD_DIM ** -0.5, F32, name="nsa_win_decode")
    o_nsa = nsa_combine(o_cmp.reshape(bs, rn, HEAD_DIM), o_slc, o_win, gates.reshape(bs, rn, 3))

    as_kv = lambda a: a.reshape(bs, ts, NSA_KV_HEADS, HEAD_DIM)
    keep = min(NSA_WINDOW, past + ts)
    state = (dk.reshape(bs, ts, DIFF_HEADS, HEAD_DIM), dv.reshape(bs, ts, DIFF_HEADS, HEAD_DIM),
             as_kv(kc), as_kv(vc), as_kv(ks), as_kv(vs),
             jnp.concatenate([win_k, as_kv(kw)], axis=1)[:, w_buf + ts - keep:],
             jnp.concatenate([win_v, as_kv(vw)], axis=1)[:, w_buf + ts - keep:])
    return o_diff.reshape(bs * ts, hd), o_nsa.reshape(bs * ts, NSA_HEADS * HEAD_DIM), state


def _odd_sample(proj, bs, ts, page_table, cache_k, cache_v, tab):
    n_pool = cache_k.shape[0]
    n_pages = page_table.shape[1]
    past = n_pages * PAGE_SIZE
    hd = MOBA_HEADS * HEAD_DIM
    p3 = proj.reshape(bs, ts, 3 * hd)
    q, k, v = [p3[..., i * hd:(i + 1) * hd] for i in range(3)]
    r = ts * MOBA_HEADS
    row_t = np.repeat(np.arange(ts), MOBA_HEADS)
    row_h = np.tile(np.arange(MOBA_HEADS), ts)
    bias = lambda kpos: _decode_bias(tab, row_t, row_h, row_h, MOBA_HEADS, past, kpos)
    page_pos = lambda p: p * PAGE_SIZE + np.arange(PAGE_SIZE)
    new_pos = np.where(np.arange(LANES // MOBA_HEADS) < ts, past + np.arange(LANES // MOBA_HEADS), -1)
    tiles = jnp.stack([bias(page_pos(1)), bias(page_pos(n_pages - 1))])
    o = moba_decode(page_table, q.reshape(bs, r, HEAD_DIM), _pad_rows(k.reshape(bs, r, HEAD_DIM), LANES),
                    _pad_rows(v.reshape(bs, r, HEAD_DIM), LANES), bias(new_pos), tiles,
                    cache_k.reshape(n_pool, PAGE_SIZE * MOBA_HEADS, HEAD_DIM),
                    cache_v.reshape(n_pool, PAGE_SIZE * MOBA_HEADS, HEAD_DIM))
    state = (k.reshape(bs, ts, MOBA_HEADS, HEAD_DIM), v.reshape(bs, ts, MOBA_HEADS, HEAD_DIM))
    return o.reshape(bs * ts, hd), state


def kernel(x_prompt, x_sample, c_prompt, c_sample, page_table, cache_diff_k, cache_diff_v, cache_nsa_cmp_k, cache_nsa_cmp_v, cache_nsa_slc_k, cache_nsa_slc_v, state_nsa_win_k, state_nsa_win_v, cache_moba_k, cache_moba_v, w_in_ab, w_out_ab, diff_lambda, diff_subln_g, nsa_cmp_pe, nsa_cmp_w1, nsa_cmp_w2, w_in_c, w_out_c, rel_bias, ada_w, ada_b, ln_g, ln_b, router_group_w, router_group_b, router_expert_w, router_expert_b, moe_w_gate, moe_w_up, moe_w_down):
    b, s, d = x_prompt.shape
    bs, ts, _ = x_sample.shape
    mp, ms = b * s, bs * ts
    depth = ada_w.shape[0]
    mod = ada_mod(jnp.concatenate([c_prompt, c_sample]), ada_w.reshape(2 * depth, d, 3 * d), ada_b.reshape(2 * depth, 3 * d))

    def mod_vectors(sub):
        per_tok = jnp.repeat(mod[sub, b:], ts, axis=0)
        return ([mod[sub, :b, None, i * d:(i + 1) * d] for i in range(3)],
                [per_tok[None, :, i * d:(i + 1) * d] for i in range(3)])

    xp, xs = x_prompt.reshape(mp, d), x_sample.reshape(ms, d)
    tiles, far = _toeplitz_tiles(rel_bias, ATT_TILE), _far_bias(rel_bias)
    even_p, even_s, odd_p, odd_s = [], [], [], []
    for l in range(depth):
        i = l // 2
        (shp, scp, gp), (shs, scs, gs) = mod_vectors(2 * l)
        hp = mod_cast(xp, shp, scp, ROW_TILE)
        hs = mod_cast(xs, shs, scs, ms)
        if l % 2 == 0:
            lam_init = 0.8 - 0.6 * math.exp(-0.3 * l)
            w1, w2 = nsa_cmp_w1[i].astype(BF16), nsa_cmp_w2[i].astype(BF16)
            proj_s, gates_s = _in_proj(hs, w_in_ab[i], 512)
            oa_p, ob_p, st_p = _even_prompt(hp, w_in_ab[i], b, s, diff_lambda[i], diff_subln_g[i], lam_init,
                                            nsa_cmp_pe[i], w1, w2, rel_bias, tiles, far)
            oa_s, ob_s, st_s = _even_sample(proj_s, gates_s, bs, ts, page_table, cache_diff_k[i], cache_diff_v[i],
                                            cache_nsa_cmp_k[i], cache_nsa_cmp_v[i], cache_nsa_slc_k[i],
                                            cache_nsa_slc_v[i], state_nsa_win_k[i], state_nsa_win_v[i],
                                            diff_lambda[i], diff_subln_g[i], lam_init, nsa_cmp_pe[i], w1, w2, rel_bias)
            even_p.append(st_p)
            even_s.append(st_s)
            w_out = w_out_ab[i].astype(BF16)
            cols = (0, 0)
        else:
            proj_s, _ = _in_proj(hs, w_in_c[i], 1024)
            hd = MOBA_HEADS * HEAD_DIM
            q_p, k_p, v_p = _in_proj_split(hp, w_in_c[i], (hd,) * 3, (b, s))
            oa_p = ob_p = moba_prompt(q_p, k_p, v_p, tiles, far).reshape(mp, hd)
            odd_p.append((k_p.reshape(b, s, MOBA_HEADS, HEAD_DIM), v_p.reshape(b, s, MOBA_HEADS, HEAD_DIM)))
            oa_s, st_s = _odd_sample(proj_s, bs, ts, page_table, cache_moba_k[i], cache_moba_v[i], rel_bias)
            ob_s = oa_s
            odd_s.append(st_s)
            w_out = w_out_c[i].astype(BF16)
            cols = (0, 1)
        xp = out_proj_norm(oa_p, ob_p, cols[0], cols[1], w_out, xp, gp, ln_g[l, 0], ln_b[l, 0], 256)
        xs = out_proj_norm(oa_s, ob_s, cols[0], cols[1], w_out, xs, gs, ln_g[l, 0], ln_b[l, 0], ms)

        (shp, scp, gp), (shs, scs, gs) = mod_vectors(2 * l + 1)
        hp = mod_cast(xp, shp, scp, ROW_TILE)
        hs = mod_cast(xs, shs, scs, ms)
        rw, rb = _router_pack(router_group_w[l], router_group_b[l], router_expert_w[l], router_expert_b[l])
        wg, wu, wd = cast_bf16(moe_w_gate[l]), cast_bf16(moe_w_up[l]), cast_bf16(moe_w_down[l])
        xp = moe_norm(hp, rw, rb, wg, wu, wd, xp, gp, ln_g[l, 1], ln_b[l, 1], ROW_TILE)
        xs = moe_norm(hs, rw, rb, wg, wu, wd, xs, gs, ln_g[l, 1], ln_b[l, 1], ms)

    stack = lambda rows, j: jnp.stack([r[j] for r in rows])
    outs = [xp.reshape(b, s, d), xs.reshape(bs, ts, d)]
    for j in range(8):
        outs += [stack(even_p, j), stack(even_s, j)]
    for j in range(2):
        outs += [stack(odd_p, j), stack(odd_s, j)]
    return tuple(outs)
```

```python
import functools
import math

import numpy as np
import jax
import jax.numpy as jnp
from jax import lax
from jax.experimental import pallas as pl
from jax.experimental.pallas import tpu as pltpu

F32 = jnp.float32
BF16 = jnp.bfloat16
NEG = -1e30

HEAD_DIM = 128
DIFF_HEADS = 8
DIFF_DK = 64
NSA_HEADS = 8
NSA_KV_HEADS = 2
NSA_GROUP = 4
NSA_CMP_LEN = 32
NSA_CMP_STRIDE = 16
NSA_CMP_HIDDEN = 256
NSA_SEL_BLOCK = 64
NSA_SEL_TOPK = 16
NSA_WINDOW = 512
MOBA_HEADS = 16
MOBA_BLOCK = 256
MOBA_TOPK = 3
PAGE_SIZE = 128
REL_BUCKETS = 32
REL_MAX_DIST = 128
N_GROUPS = 4
EXPERTS_PER_GROUP = 4
N_EXPERTS = 16
DEPTH = 2
DEEPNORM_ALPHA = (2 * DEPTH) ** 0.25
LN_EPS = 1e-5
RMS_EPS = 1e-6

LANES = 128
ATT_TILE = 256
VMEM_LIMIT = 56 * 1024 * 1024


def _params(n_axes, vmem=VMEM_LIMIT):
    return pltpu.CompilerParams(dimension_semantics=("arbitrary",) * n_axes, vmem_limit_bytes=vmem)


def _dot(a, b):
    return jnp.dot(a, b, preferred_element_type=F32)


def _dot_nt(a, b):
    return lax.dot_general(a, b, (((1,), (1,)), ((), ())), preferred_element_type=F32)


def _ada_kernel(c_ref, w_ref, b_ref, o_ref):
    c = c_ref[...]
    a = (c * jax.nn.sigmoid(c)).astype(BF16)
    o_ref[...] = _dot(a, w_ref[...].astype(BF16)) + b_ref[...]


def ada_mod(c_all, ada_w, ada_b):
    n_sub, d, d3 = ada_w.shape
    bc = c_all.shape[0]
    tn = 768
    return pl.pallas_call(
        _ada_kernel,
        grid=(n_sub, d3 // tn),
        in_specs=[pl.BlockSpec((bc, d), lambda s, j: (0, 0)),
                  pl.BlockSpec((None, d, tn), lambda s, j: (s, 0, j)),
                  pl.BlockSpec((None, 1, tn), lambda s, j: (s, 0, j))],
        out_specs=pl.BlockSpec((None, bc, tn), lambda s, j: (s, 0, j)),
        out_shape=jax.ShapeDtypeStruct((n_sub, bc, d3), F32),
        compiler_params=_params(2),
        name="ada_mod",
    )(c_all, ada_w, ada_b.reshape(n_sub, 1, d3))


def _seg_spec(vec, m, tm):
    g, tg, d = vec.shape
    if tg == 1:
        per = m // g // tm
        return pl.BlockSpec((None, 1, d), lambda i, *_: (i // per, 0, 0))
    return pl.BlockSpec((None, tm, d), lambda i, *_: (0, i, 0))


def _mod_kernel(x_ref, sh_ref, sc_ref, o_ref):
    o_ref[...] = (x_ref[...] * (1 + sc_ref[...]) + sh_ref[...]).astype(BF16)


def mod_cast(x, shift, scale, tm):
    m, d = x.shape
    return pl.pallas_call(
        _mod_kernel,
        grid=(m // tm,),
        in_specs=[pl.BlockSpec((tm, d), lambda i: (i, 0)), _seg_spec(shift, m, tm), _seg_spec(scale, m, tm)],
        out_specs=pl.BlockSpec((tm, d), lambda i: (i, 0)),
        out_shape=jax.ShapeDtypeStruct((m, d), BF16),
        compiler_params=_params(1),
        name="mod_cast",
    )(x, shift, scale)


def _mm_kernel(a_ref, w_ref, o_ref, wb_ref):
    @pl.when(pl.program_id(1) == 0)
    def _round_weights():
        wb_ref[...] = w_ref[...].astype(BF16)

    o_ref[...] = _dot(a_ref[...], wb_ref[...]).astype(o_ref.dtype)


def matmul(a, w, n_cols, tm, tn, name, col0=0):
    m, k = a.shape
    assert col0 % tn == 0 and n_cols % tn == 0
    first = col0 // tn
    return pl.pallas_call(
        _mm_kernel,
        grid=(n_cols // tn, m // tm),
        in_specs=[pl.BlockSpec((tm, k), lambda j, i: (i, 0)), pl.BlockSpec((k, tn), lambda j, i: (0, first + j))],
        out_specs=pl.BlockSpec((tm, tn), lambda j, i: (i, j)),
        out_shape=jax.ShapeDtypeStruct((m, n_cols), F32),
        scratch_shapes=[pltpu.VMEM((k, tn), BF16)],
        compiler_params=_params(2),
        name=name,
    )(a, w)


def _cast_kernel(x_ref, o_ref):
    o_ref[...] = x_ref[...].astype(o_ref.dtype)


def cast_bf16(w, layer):
    _, e, r, c = w.shape
    return pl.pallas_call(
        _cast_kernel,
        grid=(e,),
        in_specs=[pl.BlockSpec((None, None, r, c), lambda i: (layer, i, 0, 0))],
        out_specs=pl.BlockSpec((None, r, c), lambda i: (i, 0, 0)),
        out_shape=jax.ShapeDtypeStruct((e, r, c), BF16),
        compiler_params=_params(1),
        name="cast_bf16",
    )(w)


def _postnorm(x, y, gate, g, b):
    z = DEEPNORM_ALPHA * x + (1 + gate) * y
    zc = z - jnp.mean(z, -1, keepdims=True)
    var = jnp.mean(zc * zc, -1, keepdims=True)
    return zc * lax.rsqrt(var + LN_EPS) * g + b


def _outln_kernel(oa_ref, ob_ref, w_ref, x_ref, gate_ref, g_ref, b_ref, out_ref):
    half = oa_ref.shape[1]
    y = _dot(oa_ref[...], w_ref[:half, :]) + _dot(ob_ref[...], w_ref[half:, :])
    out_ref[...] = _postnorm(x_ref[...], y, gate_ref[...], g_ref[...], b_ref[...])


def out_proj_norm(oa, ob, cols_a, cols_b, w_out, x, gate, ln_g, ln_b, tm):
    m, d = x.shape
    half = w_out.shape[0] // 2
    return pl.pallas_call(
        _outln_kernel,
        grid=(m // tm,),
        in_specs=[pl.BlockSpec((tm, half), lambda i: (i, cols_a)),
                  pl.BlockSpec((tm, half), lambda i: (i, cols_b)),
                  pl.BlockSpec(w_out.shape, lambda i: (0, 0)),
                  pl.BlockSpec((tm, d), lambda i: (i, 0)),
                  _seg_spec(gate, m, tm),
                  pl.BlockSpec((1, d), lambda i: (0, 0)),
                  pl.BlockSpec((1, d), lambda i: (0, 0))],
        out_specs=pl.BlockSpec((tm, d), lambda i: (i, 0)),
        out_shape=jax.ShapeDtypeStruct((m, d), F32),
        compiler_params=_params(1),
        name="out_proj_norm",
    )(oa, ob, w_out, x, gate, ln_g.reshape(1, d), ln_b.reshape(1, d))


def _first_argmax(v, lane, valid):
    vm = jnp.where(valid, v, -jnp.inf)
    top = jnp.max(vm, -1, keepdims=True)
    idx = jnp.min(jnp.where(valid & (vm == top), lane, 1 << 20), -1, keepdims=True)
    return top, idx


def _moe_kernel(h_ref, rw_ref, rb_ref, wg_ref, wu_ref, wd_ref, x_ref, gate_ref, g_ref, b_ref, out_ref,
                comb_ref, acc_ref):
    e = pl.program_id(1)
    h = h_ref[...]

    @pl.when(e == 0)
    def _route():
        logits = _dot(h, rw_ref[...]) + rb_ref[...]
        lane = lax.broadcasted_iota(jnp.int32, logits.shape, 1)
        is_g = lane < N_GROUPS
        gmax = jnp.max(jnp.where(is_g, logits, -jnp.inf), -1, keepdims=True)
        gex = jnp.where(is_g, jnp.exp(logits - gmax), 0.0)
        pg = gex / jnp.sum(gex, -1, keepdims=True)
        pg_top, g_top = _first_argmax(pg, lane, is_g)
        ex_id = lane - N_GROUPS
        in_grp = (ex_id >= g_top * EXPERTS_PER_GROUP) & (ex_id < (g_top + 1) * EXPERTS_PER_GROUP)
        emax = jnp.max(jnp.where(in_grp, logits, -jnp.inf), -1, keepdims=True)
        eex = jnp.where(in_grp, jnp.exp(logits - emax), 0.0)
        pe = eex / jnp.sum(eex, -1, keepdims=True)
        p1, i1 = _first_argmax(pe, lane, in_grp)
        p2, i2 = _first_argmax(pe, lane, in_grp & (lane != i1))
        tot = p1 + p2
        comb = jnp.where(lane == i1, p1 / tot * pg_top, 0.0) + jnp.where(lane == i2, p2 / tot * pg_top, 0.0)
        comb_ref[...] = comb
        acc_ref[...] = jnp.zeros_like(acc_ref)

    lane = lax.broadcasted_iota(jnp.int32, comb_ref.shape, 1)
    c_e = jnp.sum(jnp.where(lane == e + N_GROUPS, comb_ref[...], 0.0), -1, keepdims=True)
    a = _dot(h, wg_ref[...])
    u = _dot(h, wu_ref[...])
    hid = (a * jax.nn.sigmoid(a)) * u * c_e
    acc_ref[...] += _dot(hid.astype(BF16), wd_ref[...])

    @pl.when(e == pl.num_programs(1) - 1)
    def _finish():
        out_ref[...] = _postnorm(x_ref[...], acc_ref[...], gate_ref[...], g_ref[...], b_ref[...])


def _router_pack(wg_r, bg_r, we_r, be_r):
    d = wg_r.shape[0]
    pad = LANES - N_GROUPS - N_EXPERTS
    rw = jnp.concatenate([wg_r, we_r, jnp.zeros((d, pad), F32)], axis=1).astype(BF16)
    rb = jnp.concatenate([bg_r, be_r, jnp.zeros((pad,), F32)]).reshape(1, LANES)
    return rw, rb


def moe_norm(h, rw, rb, wg, wu, wd, x, gate, ln_g, ln_b, tm):
    m, d = x.shape
    n_e, _, f = wg.shape
    return pl.pallas_call(
        _moe_kernel,
        grid=(m // tm, n_e),
        in_specs=[pl.BlockSpec((tm, d), lambda i, e: (i, 0)),
                  pl.BlockSpec(rw.shape, lambda i, e: (0, 0)),
                  pl.BlockSpec(rb.shape, lambda i, e: (0, 0)),
                  pl.BlockSpec((None, d, f), lambda i, e: (e, 0, 0)),
                  pl.BlockSpec((None, d, f), lambda i, e: (e, 0, 0)),
                  pl.BlockSpec((None, f, d), lambda i, e: (e, 0, 0)),
                  pl.BlockSpec((tm, d), lambda i, e: (i, 0)),
                  _seg_spec(gate, m, tm),
                  pl.BlockSpec((1, d), lambda i, e: (0, 0)),
                  pl.BlockSpec((1, d), lambda i, e: (0, 0))],
        out_specs=pl.BlockSpec((tm, d), lambda i, e: (i, 0)),
        out_shape=jax.ShapeDtypeStruct((m, d), F32),
        scratch_shapes=[pltpu.VMEM((tm, LANES), F32), pltpu.VMEM((tm, d), F32)],
        compiler_params=_params(2),
        name="moe_norm",
    )(h, rw, rb, wg, wu, wd, x, gate, ln_g.reshape(1, d), ln_b.reshape(1, d))


def _t5_bucket(dist):
    n = jnp.maximum(dist, 0)
    max_exact = REL_BUCKETS // 2
    nf = jnp.maximum(n, 1).astype(F32)
    log_b = max_exact + (jnp.log(nf / max_exact) / math.log(REL_MAX_DIST / max_exact) * (REL_BUCKETS - max_exact)).astype(jnp.int32)
    return jnp.where(n < max_exact, n, jnp.minimum(log_b, REL_BUCKETS - 1))


def _lookup(cols, bucket):
    out = jnp.zeros(jnp.broadcast_shapes(cols.shape[1:], bucket.shape), F32)
    for k in range(REL_BUCKETS):
        out = jnp.where(bucket == k, cols[k], out)
    return out


def _bias_of(tab, dist, mask):
    cols = tab.astype(F32).reshape(tab.shape + (1,) * dist.ndim)
    return jnp.where(mask, _lookup(cols, _t5_bucket(dist)), NEG)


def _toeplitz_tiles(tab, t):
    i = jnp.arange(t)[:, None]
    j = jnp.arange(t)[None, :]
    diag = _bias_of(tab, i - j, i >= j)
    left = _bias_of(tab, t + i - j, jnp.ones((t, t), bool))
    return jnp.stack([diag, left], axis=1)


def _far_bias(tab):
    return jnp.broadcast_to(tab[REL_BUCKETS - 1].astype(F32)[:, None, None], (tab.shape[1], 1, LANES))


def _attend_blocks(q, blocks, s_ref, m_floor=None):
    m_el = None
    for i, (load_k, _, logits_fn) in enumerate(blocks):
        s = logits_fn(_dot_nt(q, load_k()))
        s_ref[i] = s
        m_el = s if m_el is None else jnp.maximum(m_el, s)
    m = jnp.max(m_el, -1, keepdims=True)
    if m_floor is not None:
        m = jnp.maximum(m, m_floor)
    p_sum = acc = None
    for i, (_, load_v, _) in enumerate(blocks):
        p = jnp.exp(s_ref[i] - m)
        p_sum = p if p_sum is None else p_sum + p
        d = _dot(p.astype(BF16), load_v())
        acc = d if acc is None else acc + d
    return m, jnp.sum(p_sum, -1, keepdims=True), acc


def _per_query_tile(qt, n_tiles, body):
    for n in range(1, n_tiles + 1):
        pl.when(qt == n - 1)(functools.partial(body, n))


def _cast_once(qt, pairs):
    @pl.when(qt == 0)
    def _cast():
        for src, dst in pairs:
            dst[...] = src[...].astype(BF16)


def _causal_blocks(n, t, kb_ref, vb_ref, diag_fn, left_fn, far_fn):
    blocks = []
    for kt in range(n):
        fn = diag_fn if kt == n - 1 else (left_fn if kt == n - 2 else far_fn)
        blocks.append((lambda kt=kt: kb_ref[kt * t:(kt + 1) * t, :], lambda kt=kt: vb_ref[kt * t:(kt + 1) * t, :],
                       functools.partial(fn, kt)))
    return blocks


def _diff_prompt_kernel(q_ref, k_ref, v_ref, bias_ref, far_ref, lam_ref, sub_ref, o_ref, kb_ref, vb_ref, s_ref,
                        *, lam_init):
    t = q_ref.shape[0]
    qt = pl.program_id(2)
    _cast_once(qt, ((k_ref, kb_ref), (v_ref, vb_ref)))
    q = q_ref[...]
    lane = lax.broadcasted_iota(jnp.int32, q.shape, 1)
    q1 = jnp.where(lane < DIFF_DK, q, 0.0).astype(BF16)
    q2 = jnp.where(lane >= DIFF_DK, q, 0.0).astype(BF16)
    scale = DIFF_DK ** -0.5
    far = far_ref[...][:, :1]
    lam = lam_ref[...]
    lam_full = (jnp.exp(jnp.sum(lam[0:1] * lam[1:2], keepdims=True))
                - jnp.exp(jnp.sum(lam[2:3] * lam[3:4], keepdims=True)) + lam_init)

    def body(n):
        blocks = _causal_blocks(n, t, kb_ref, vb_ref,
                                lambda kt, s: s * scale + bias_ref[0],
                                lambda kt, s: s * scale + bias_ref[1],
                                lambda kt, s: s * scale + far)
        _, l1, a1 = _attend_blocks(q1, blocks, s_ref.at[0])
        _, l2, a2 = _attend_blocks(q2, blocks, s_ref.at[1])
        o = a1 / l1 - lam_full * (a2 / l2)
        o = o * lax.rsqrt(jnp.mean(o * o, -1, keepdims=True) + RMS_EPS) * sub_ref[...] * (1.0 - lam_init)
        o_ref[...] = o.astype(o_ref.dtype)

    _per_query_tile(qt, k_ref.shape[0] // t, body)


def diff_prompt(q, k, v, tiles, far, lam, subln, lam_init):
    b, s, _ = q.shape
    t = ATT_TILE
    kern = functools.partial(_diff_prompt_kernel, lam_init=lam_init)
    return pl.pallas_call(
        kern,
        grid=(b, DIFF_HEADS, s // t),
        in_specs=[pl.BlockSpec((None, t, HEAD_DIM), lambda bi, h, qt: (bi, qt, h)),
                  pl.BlockSpec((None, s, HEAD_DIM), lambda bi, h, qt: (bi, 0, h)),
                  pl.BlockSpec((None, s, HEAD_DIM), lambda bi, h, qt: (bi, 0, h)),
                  pl.BlockSpec((None, 2, t, t), lambda bi, h, qt: (h, 0, 0, 0)),
                  pl.BlockSpec((None, 1, LANES), lambda bi, h, qt: (h, 0, 0)),
                  pl.BlockSpec(lam.shape, lambda bi, h, qt: (0, 0)),
                  pl.BlockSpec((1, HEAD_DIM), lambda bi, h, qt: (0, 0))],
        out_specs=pl.BlockSpec((None, t, HEAD_DIM), lambda bi, h, qt: (bi, qt, h)),
        out_shape=jax.ShapeDtypeStruct((b, s, DIFF_HEADS * HEAD_DIM), BF16),
        scratch_shapes=[pltpu.VMEM((s, HEAD_DIM), BF16), pltpu.VMEM((s, HEAD_DIM), BF16),
                        pltpu.VMEM((2, s // t, t, t), F32)],
        compiler_params=_params(3),
        name="diff_prompt",
    )(q, k, v, tiles, far, lam, subln.reshape(1, HEAD_DIM))


def _cmp_ab_kernel(x_ref, pe_ref, w1_ref, o_ref, *, paged):
    half = w1_ref.shape[0] // 2
    hid = w1_ref.shape[1]
    per = PAGE_SIZE // NSA_CMP_STRIDE
    for g in range(NSA_KV_HEADS):
        if paged:
            n_rows = x_ref.shape[0] * per
            xg = jnp.concatenate(
                [x_ref[:, pl.ds(NSA_KV_HEADS * p + g, per, stride=NSA_KV_HEADS * NSA_CMP_STRIDE), :].reshape(n_rows, HEAD_DIM)
                 for p in range(NSA_CMP_STRIDE)], axis=1)
        else:
            xg = jnp.concatenate([x_ref[:, (NSA_KV_HEADS * p + g) * HEAD_DIM:(NSA_KV_HEADS * p + g + 1) * HEAD_DIM]
                                  for p in range(NSA_CMP_STRIDE)], axis=1)
        top = (xg + pe_ref[0:1, :]).astype(BF16)
        bot = (xg + pe_ref[1:2, :]).astype(BF16)
        o_ref[:, 2 * g * hid:(2 * g + 1) * hid] = _dot(top, w1_ref[:half, :])
        o_ref[:, (2 * g + 1) * hid:(2 * g + 2) * hid] = _dot(bot, w1_ref[half:, :])


def _cmp_fin_kernel(*refs, n_prefetch, nseq):
    ab_refs, w2_ref, o_ref = refs[n_prefetch:-2], refs[-2], refs[-1]
    per_seq = len(ab_refs) // nseq
    hid = w2_ref.shape[0]
    for s in range(nseq):
        mine = ab_refs[s * per_seq:(s + 1) * per_seq]
        ab = jnp.concatenate([r[...] for r in mine], axis=0) if per_seq > 1 else mine[0][...]
        n = ab.shape[0]
        row = lax.broadcasted_iota(jnp.int32, (n, HEAD_DIM), 0)
        for g in range(NSA_KV_HEADS):
            a = ab[:, 2 * g * hid:(2 * g + 1) * hid]
            b = ab[:, (2 * g + 1) * hid:(2 * g + 2) * hid]
            hidden = jax.nn.gelu(a + pltpu.roll(b, n - 1, 0), approximate=True)
            out = _dot(hidden.astype(BF16), w2_ref[...])
            o_ref[s, g] = jnp.where(row < n - 1, out, 0.0)


def nsa_compress(kseq, pe, w1, w2, page_table):
    rows = kseq.shape[0] * kseq.shape[1] // NSA_CMP_STRIDE
    width = NSA_CMP_STRIDE * NSA_KV_HEADS * HEAD_DIM
    hid = w1.shape[1]
    tr = math.gcd(rows, 512)
    per = PAGE_SIZE // NSA_CMP_STRIDE
    if page_table is None:
        x = kseq.reshape(rows, width)
        x_spec = pl.BlockSpec((tr, width), lambda i: (i, 0))
    else:
        x = kseq.reshape(kseq.shape[0], PAGE_SIZE * NSA_KV_HEADS, HEAD_DIM)
        x_spec = pl.BlockSpec((tr // per, PAGE_SIZE * NSA_KV_HEADS, HEAD_DIM), lambda i: (i, 0, 0))
    ab = pl.pallas_call(
        functools.partial(_cmp_ab_kernel, paged=page_table is not None),
        grid=(rows // tr,),
        in_specs=[x_spec,
                  pl.BlockSpec((2, width // 2), lambda i: (0, 0)),
                  pl.BlockSpec(w1.shape, lambda i: (0, 0))],
        out_specs=pl.BlockSpec((tr, 4 * hid), lambda i: (i, 0)),
        out_shape=jax.ShapeDtypeStruct((rows, 4 * hid), F32),
        compiler_params=_params(1),
        name="nsa_cmp_layer1",
    )(x, pe.reshape(2, width // 2), w1)
    n_cmp = 128
    if page_table is None:
        nb = kseq.shape[0]
        out_block = pl.BlockSpec((1, NSA_KV_HEADS, n_cmp, HEAD_DIM), lambda b: (b, 0, 0, 0))
        return pl.pallas_call(
            functools.partial(_cmp_fin_kernel, n_prefetch=0, nseq=1),
            grid=(nb,),
            in_specs=[pl.BlockSpec((n_cmp, 4 * hid), lambda b: (b, 0)), pl.BlockSpec(w2.shape, lambda b: (0, 0))],
            out_specs=out_block,
            out_shape=jax.ShapeDtypeStruct((nb, NSA_KV_HEADS, n_cmp, HEAD_DIM), F32),
            compiler_params=_params(1),
            name="nsa_cmp_layer2",
        )(ab, w2)
    nb, n_pages = page_table.shape
    per = PAGE_SIZE // NSA_CMP_STRIDE
    ab3 = ab.reshape(rows // per, per, 4 * hid)
    nseq = math.gcd(nb, 4)
    grid_spec = pltpu.PrefetchScalarGridSpec(
        num_scalar_prefetch=1,
        grid=(nb // nseq,),
        in_specs=[pl.BlockSpec((None, per, 4 * hid), lambda b, pt, s=s, p=p: (pt[b * nseq + s, p], 0, 0))
                  for s in range(nseq) for p in range(n_pages)]
        + [pl.BlockSpec(w2.shape, lambda b, pt: (0, 0))],
        out_specs=pl.BlockSpec((nseq, NSA_KV_HEADS, n_cmp, HEAD_DIM), lambda b, pt: (b, 0, 0, 0)),
    )
    return pl.pallas_call(
        functools.partial(_cmp_fin_kernel, n_prefetch=1, nseq=nseq),
        grid_spec=grid_spec,
        out_shape=jax.ShapeDtypeStruct((nb, NSA_KV_HEADS, n_cmp, HEAD_DIM), F32),
        compiler_params=_params(1),
        name="nsa_cmp_layer2_paged",
    )(page_table, *([ab3] * (nseq * n_pages)), w2)


def _rank_select(score, lane, n_valid_lanes, topk):
    rank = jnp.zeros(score.shape, jnp.int32)
    for j in range(n_valid_lanes):
        col = score[:, j:j + 1]
        beats = (col > score) | ((col == score) & (j < lane))
        rank = rank + beats.astype(jnp.int32)
    return rank < topk


def _nsa_cmp_kernel(q_ref, kc_ref, vc_ref, bias_ref, cov_ref, o_ref, sel_ref, *, qpos0, n_slc):
    t = q_ref.shape[1]
    qt = pl.program_id(2)
    scale = HEAD_DIM ** -0.5
    lane = lax.broadcasted_iota(jnp.int32, (t, LANES), 1)
    qpos = qpos0 + qt * t + lax.broadcasted_iota(jnp.int32, (t, LANES), 0)
    cur = qpos // NSA_SEL_BLOCK
    valid = (lane * NSA_SEL_BLOCK <= qpos) & (lane < n_slc)
    forced = valid & ((lane == 0) | (lane == cur) | (lane == cur - 1))
    for i in range(q_ref.shape[0]):
        kc = kc_ref[i].astype(BF16)
        vc = vc_ref[i].astype(BF16)
        imp = jnp.zeros((t, LANES), F32)
        for r in range(NSA_GROUP):
            qr = q_ref[i, :, r * HEAD_DIM:(r + 1) * HEAD_DIM].astype(BF16)
            bias = bias_ref[r]
            live = bias > 0.5 * NEG
            s = _dot_nt(qr, kc) * scale + bias
            m = jnp.max(s, -1, keepdims=True)
            p = jnp.where(live, jnp.exp(s - m), 0.0)
            p = p / jnp.maximum(jnp.sum(p, -1, keepdims=True), jnp.finfo(F32).tiny)
            pb = p.astype(BF16)
            o_ref[i, :, r * HEAD_DIM:(r + 1) * HEAD_DIM] = _dot(pb, vc)
            imp = imp + _dot(pb, cov_ref[...])
        score = jnp.where(forced, jnp.inf, jnp.where(valid, imp, -jnp.inf))
        chosen = _rank_select(score, lane, n_slc, min(NSA_SEL_TOPK, n_slc))
        sel_ref[i] = (valid & chosen).astype(F32)


def nsa_cmp_select(q_arr, q_col0, t, kcmp, vcmp, tab_n, qpos0, seq_len):
    b, sq, _ = q_arr.shape
    n_cmp = (seq_len - NSA_CMP_LEN) // NSA_CMP_STRIDE + 1
    n_slc = -(-seq_len // NSA_SEL_BLOCK)
    i = jnp.arange(LANES)
    qpos = qpos0 + jnp.arange(sq)
    dist = qpos[:, None] - (i * NSA_CMP_STRIDE + NSA_CMP_LEN - 1)[None, :]
    bias = _bias_of(tab_n, dist, (dist >= 0) & (i < n_cmp)[None, :]).reshape(NSA_KV_HEADS, NSA_GROUP, sq, LANES)
    cs = np.arange(LANES)[:, None] * NSA_CMP_STRIDE
    ss = np.arange(LANES)[None, :] * NSA_SEL_BLOCK
    covers = ((cs < ss + NSA_SEL_BLOCK) & (cs + NSA_CMP_LEN > ss)
              & (np.arange(LANES)[:, None] < n_cmp) & (np.arange(LANES)[None, :] < n_slc))
    covers = jnp.asarray(covers, BF16)
    kern = functools.partial(_nsa_cmp_kernel, qpos0=qpos0, n_slc=n_slc)
    gw = NSA_GROUP * HEAD_DIM
    nseq = math.gcd(b, 8) if sq == t else 1
    return pl.pallas_call(
        kern,
        grid=(b // nseq, NSA_KV_HEADS, sq // t),
        in_specs=[pl.BlockSpec((nseq, t, gw), lambda bi, g, qt: (bi, qt, q_col0 + g)),
                  pl.BlockSpec((nseq, None, LANES, HEAD_DIM), lambda bi, g, qt: (bi, g, 0, 0)),
                  pl.BlockSpec((nseq, None, LANES, HEAD_DIM), lambda bi, g, qt: (bi, g, 0, 0)),
                  pl.BlockSpec((None, NSA_GROUP, t, LANES), lambda bi, g, qt: (g, 0, qt, 0)),
                  pl.BlockSpec((LANES, LANES), lambda bi, g, qt: (0, 0))],
        out_specs=[pl.BlockSpec((nseq, t, gw), lambda bi, g, qt: (bi, qt, g)),
                   pl.BlockSpec((nseq, None, t, LANES), lambda bi, g, qt: (bi, g, qt, 0))],
        out_shape=[jax.ShapeDtypeStruct((b, sq, NSA_KV_HEADS * gw), F32),
                   jax.ShapeDtypeStruct((b, NSA_KV_HEADS, sq, LANES), F32)],
        compiler_params=_params(3),
        name="nsa_cmp_select",
    )(q_arr, kcmp, vcmp, bias, covers)


def _gate_pad(g):
    lead = g.shape[:-1]
    g = g.reshape(*lead, NSA_KV_HEADS, NSA_GROUP * 3)
    g = jnp.pad(g, [(0, 0)] * (len(lead) + 1) + [(0, LANES - NSA_GROUP * 3)])
    return g.reshape(*lead, NSA_KV_HEADS * LANES)


def _nsa_prompt_kernel(q_ref, ks_ref, vs_ref, kw_ref, vw_ref, sel_ref, ocmp_ref, g_ref, bias_ref, far_ref, e_ref,
                       o_ref, ksb_ref, vsb_ref, kwb_ref, vwb_ref, s_ref, sw_ref):
    t = q_ref.shape[0]
    qt = pl.program_id(2)
    scale = HEAD_DIM ** -0.5
    _cast_once(qt, ((ks_ref, ksb_ref), (vs_ref, vsb_ref), (kw_ref, kwb_ref), (vw_ref, vwb_ref)))
    q4 = jnp.concatenate([q_ref[:, r * HEAD_DIM:(r + 1) * HEAD_DIM] for r in range(NSA_GROUP)], axis=0).astype(BF16)
    sel4 = jnp.concatenate([sel_ref[...].astype(BF16)] * NSA_GROUP, axis=0)
    far = far_ref[...][:, :1]
    rows = NSA_GROUP * t

    def chosen(kt):
        return _dot(sel4, e_ref[:, kt * t:(kt + 1) * t]) > 0.5

    def body(n):
        slc = _causal_blocks(n, t, ksb_ref, vsb_ref,
                             lambda kt, s: jnp.where(chosen(kt), s * scale + bias_ref[0], NEG),
                             lambda kt, s: jnp.where(chosen(kt), s * scale + bias_ref[1], NEG),
                             lambda kt, s: jnp.where(chosen(kt), s * scale + far, NEG))
        _, l_s, a_s = _attend_blocks(q4, slc, s_ref)

        def edge(kt, s):
            query = lax.broadcasted_iota(jnp.int32, (rows, t), 0) & (t - 1)
            key = lax.broadcasted_iota(jnp.int32, (rows, t), 1)
            return jnp.where(key > query, s * scale + far, NEG)

        win = _causal_blocks(n, t, kwb_ref, vwb_ref,
                             lambda kt, s: s * scale + bias_ref[0],
                             lambda kt, s: s * scale + bias_ref[1],
                             edge)[max(n - 3, 0):]
        _, l_w, a_w = _attend_blocks(q4, win, sw_ref)
        o_slc = a_s / l_s
        o_win = a_w / l_w
        gate = jax.nn.sigmoid(g_ref[...])
        for r in range(NSA_GROUP):
            o = (gate[:, 3 * r:3 * r + 1] * ocmp_ref[:, r * HEAD_DIM:(r + 1) * HEAD_DIM]
                 + gate[:, 3 * r + 1:3 * r + 2] * o_slc[r * t:(r + 1) * t]
                 + gate[:, 3 * r + 2:3 * r + 3] * o_win[r * t:(r + 1) * t])
            o_ref[:, r * HEAD_DIM:(r + 1) * HEAD_DIM] = o.astype(o_ref.dtype)

    _per_query_tile(qt, ks_ref.shape[0] // t, body)


def nsa_prompt(nq, ks, vs, kw, vw, sel, o_cmp, gates, tiles, far):
    b, s, _ = nq.shape
    t = ATT_TILE
    assert 2 * t == NSA_WINDOW
    nkt = s // t
    gw = NSA_GROUP * HEAD_DIM
    expand = jnp.asarray(np.arange(LANES)[:, None] == np.arange(s)[None, :] // NSA_SEL_BLOCK, BF16)
    rows = NSA_GROUP * t
    tiles4 = jnp.swapaxes(tiles.reshape(NSA_KV_HEADS, NSA_GROUP, 2, t, t), 1, 2).reshape(NSA_KV_HEADS, 2, rows, t)
    far4 = jnp.broadcast_to(far.reshape(NSA_KV_HEADS, NSA_GROUP, 1, LANES),
                            (NSA_KV_HEADS, NSA_GROUP, t, LANES)).reshape(NSA_KV_HEADS, rows, LANES)
    kv = pl.BlockSpec((None, s, HEAD_DIM), lambda bi, g, qt: (bi, 0, g))
    return pl.pallas_call(
        _nsa_prompt_kernel,
        grid=(b, NSA_KV_HEADS, nkt),
        in_specs=[pl.BlockSpec((None, t, gw), lambda bi, g, qt: (bi, qt, g)),
                  kv, kv, kv, kv,
                  pl.BlockSpec((None, None, t, LANES), lambda bi, g, qt: (bi, g, qt, 0)),
                  pl.BlockSpec((None, t, gw), lambda bi, g, qt: (bi, qt, g)),
                  pl.BlockSpec((None, t, LANES), lambda bi, g, qt: (bi, qt, g)),
                  pl.BlockSpec((None, 2, rows, t), lambda bi, g, qt: (g, 0, 0, 0)),
                  pl.BlockSpec((None, rows, LANES), lambda bi, g, qt: (g, 0, 0)),
                  pl.BlockSpec((LANES, s), lambda bi, g, qt: (0, 0))],
        out_specs=pl.BlockSpec((None, t, gw), lambda bi, g, qt: (bi, qt, g)),
        out_shape=jax.ShapeDtypeStruct((b, s, NSA_KV_HEADS * gw), BF16),
        scratch_shapes=[pltpu.VMEM((s, HEAD_DIM), BF16)] * 4
        + [pltpu.VMEM((nkt, rows, t), F32), pltpu.VMEM((min(nkt, 3), rows, t), F32)],
        compiler_params=_params(3),
        name="nsa_prompt",
    )(nq, ks, vs, kw, vw, sel, o_cmp, gates, tiles4, far4, expand)


def _moba_prompt_kernel(q_ref, k_ref, v_ref, bias_ref, far_ref, o_ref, kmean_ref, kb_ref, vb_ref, s_ref):
    t = q_ref.shape[0]
    qt = pl.program_id(2)
    nb = k_ref.shape[0] // t
    scale = HEAD_DIM ** -0.5
    _cast_once(qt, ((k_ref, kb_ref), (v_ref, vb_ref)))

    @pl.when(qt == 0)
    def _means():
        kmean_ref[...] = jnp.zeros_like(kmean_ref)
        for j in range(nb):
            kmean_ref[j:j + 1, :] = jnp.mean(k_ref[j * t:(j + 1) * t, :], axis=0, keepdims=True)

    q = q_ref[...].astype(BF16)
    lane = lax.broadcasted_iota(jnp.int32, (t, LANES), 1)
    score = jnp.where(lane < qt, _dot_nt(q, kmean_ref[...].astype(BF16)), -jnp.inf)
    sel = (lane < qt) & _rank_select(score, lane, nb, MOBA_TOPK)
    far = far_ref[...][:, :1]

    def body(n):
        blocks = _causal_blocks(n, t, kb_ref, vb_ref,
                                lambda kt, s: s * scale + bias_ref[0],
                                lambda kt, s: jnp.where(sel[:, kt:kt + 1], s * scale + bias_ref[1], NEG),
                                lambda kt, s: jnp.where(sel[:, kt:kt + 1], s * scale + far, NEG))
        _, l, acc = _attend_blocks(q, blocks, s_ref)
        o_ref[...] = (acc / l).astype(o_ref.dtype)

    _per_query_tile(qt, nb, body)


def moba_prompt(q, k, v, tiles, far):
    b, s, _ = q.shape
    t = ATT_TILE
    assert t == MOBA_BLOCK and s // t <= LANES
    return pl.pallas_call(
        _moba_prompt_kernel,
        grid=(b, MOBA_HEADS, s // t),
        in_specs=[pl.BlockSpec((None, t, HEAD_DIM), lambda bi, h, qt: (bi, qt, h)),
                  pl.BlockSpec((None, s, HEAD_DIM), lambda bi, h, qt: (bi, 0, h)),
                  pl.BlockSpec((None, s, HEAD_DIM), lambda bi, h, qt: (bi, 0, h)),
                  pl.BlockSpec((None, 2, t, t), lambda bi, h, qt: (h, 0, 0, 0)),
                  pl.BlockSpec((None, 1, LANES), lambda bi, h, qt: (h, 0, 0))],
        out_specs=pl.BlockSpec((None, t, HEAD_DIM), lambda bi, h, qt: (bi, qt, h)),
        out_shape=jax.ShapeDtypeStruct((b, s, MOBA_HEADS * HEAD_DIM), BF16),
        scratch_shapes=[pltpu.VMEM((LANES, HEAD_DIM), F32), pltpu.VMEM((s, HEAD_DIM), BF16),
                        pltpu.VMEM((s, HEAD_DIM), BF16), pltpu.VMEM((s // t, t, t), F32)],
        compiler_params=_params(3),
        name="moba_prompt",
    )(q, k, v, tiles, far)


def _decode_bias(tab, row_t, row_col, row_kv, n_kv, qpos0, kpos, window=None):
    c = kpos.shape[0] * n_kv
    col_kv = np.arange(c) % n_kv
    kp = np.repeat(kpos, n_kv)
    dist = (qpos0 + row_t)[:, None] - kp[None, :]
    ok = (row_kv[:, None] == col_kv[None, :]) & (dist >= 0) & (kp >= 0)[None, :]
    if window is not None:
        ok = ok & (dist < window)
    per_row = jnp.take(tab.astype(F32), jnp.asarray(row_col, jnp.int32), axis=1)
    b = _lookup(per_row[:, :, None], _t5_bucket(jnp.asarray(dist)))
    return jnp.where(jnp.asarray(ok), b, NEG)


def _decode_kernel(*refs, scale, ppc, n_pages, masked, diff_lam_init):
    pt_ref, q_ref, kn_ref, vn_ref, bnew_ref, tiles_ref = refs[:6]
    pos = 6
    rm_ref = None
    if masked:
        rm_ref = refs[pos]
        pos += 1
    lam_ref = sub_ref = None
    if diff_lam_init is not None:
        lam_ref, sub_ref = refs[pos:pos + 2]
        pos += 2
    k_refs = refs[pos:pos + ppc]
    v_refs = refs[pos + ppc:pos + 2 * ppc]
    o_ref, m_ref, l_ref, acc_ref, s_ref = refs[pos + 2 * ppc:]
    single = ppc == n_pages
    c = 0 if single else pl.program_id(1)
    q = q_ref[...].astype(BF16)
    rows = q.shape[0]
    cols = k_refs[0].shape[0]

    @pl.when(pl.program_id(1) == 0)
    def _init():
        m_ref[...] = jnp.full_like(m_ref, NEG)
        l_ref[...] = jnp.zeros_like(l_ref)
        acc_ref[...] = jnp.zeros_like(acc_ref)

    s_new = _dot_nt(q, kn_ref[...].astype(BF16)) * scale + bnew_ref[...]
    if not single:
        s_new = jnp.where(c == 0, s_new, NEG)

    def page_logits(p, s):
        gp = c * ppc + p
        if single:
            tile = tiles_ref[0 if gp == 0 else (2 if gp == n_pages - 1 else 1)]
        else:
            tile = tiles_ref[1]
            if p == 0:
                tile = jnp.where(gp == 0, tiles_ref[0], tile)
            if p == ppc - 1:
                tile = jnp.where(gp == n_pages - 1, tiles_ref[2], tile)
        s = s * scale + tile
        if masked:
            assert single
            col = lax.broadcasted_iota(jnp.int32, (rows, cols), 1)
            keep = jnp.where(col < cols // 2, rm_ref[:, 2 * gp:2 * gp + 1], rm_ref[:, 2 * gp + 1:2 * gp + 2])
            s = jnp.where(keep > 0.5, s, NEG)
        return s

    blocks = [(lambda p=p: k_refs[p][...].astype(BF16), lambda p=p: v_refs[p][...].astype(BF16),
               functools.partial(page_logits, p)) for p in range(ppc)]
    m_old = m_ref[...]
    m, l_pages, acc_pages = _attend_blocks(q, blocks, s_ref,
                                           m_floor=jnp.maximum(m_old, jnp.max(s_new, -1, keepdims=True)))
    alpha = jnp.exp(m_old - m)
    p_new = jnp.exp(s_new - m)
    l_ref[...] = alpha * l_ref[...] + l_pages + jnp.sum(p_new, -1, keepdims=True)
    acc_ref[...] = alpha * acc_ref[...] + acc_pages + _dot(p_new.astype(BF16), vn_ref[...].astype(BF16))
    m_ref[...] = m

    @pl.when(pl.program_id(1) == pl.num_programs(1) - 1)
    def _finish():
        o = acc_ref[...] / l_ref[...]
        if diff_lam_init is not None:
            lam = lam_ref[...]
            lam_full = (jnp.exp(jnp.sum(lam[0:1] * lam[1:2], keepdims=True))
                        - jnp.exp(jnp.sum(lam[2:3] * lam[3:4], keepdims=True)) + diff_lam_init)
            half = rows // 2
            o = o[:half] - lam_full * o[half:]
            o = o * lax.rsqrt(jnp.mean(o * o, -1, keepdims=True) + RMS_EPS) * sub_ref[...] * (1.0 - diff_lam_init)
        o_ref[...] = o.astype(o_ref.dtype)


def decode_attn(page_table, q, k_new, v_new, bias_new, tiles, k_pages, v_pages, n_chunks, scale, out_dtype,
                row_mask=None, diff=None, name="decode_attn"):
    bs, r, _ = q.shape
    n_pages = page_table.shape[1]
    ppc = n_pages // n_chunks
    c = k_pages.shape[1]
    per_b = lambda shape: pl.BlockSpec((None,) + shape, lambda b, ch, pt: (b, 0, 0))
    const = lambda shape: pl.BlockSpec(shape, lambda b, ch, pt: (0,) * len(shape))
    in_specs = [per_b((r, HEAD_DIM)), per_b((LANES, HEAD_DIM)), per_b((LANES, HEAD_DIM)),
                const((r, LANES)), const((3, r, c))]
    args = [q, k_new, v_new, bias_new, tiles]
    if row_mask is not None:
        in_specs.append(per_b((r, LANES)))
        args.append(row_mask)
    r_out = r
    if diff is not None:
        in_specs += [const(diff[0].shape), const((1, HEAD_DIM))]
        args += [diff[0], diff[1].reshape(1, HEAD_DIM)]
        r_out = r // 2
    page = lambda p: pl.BlockSpec((None, c, HEAD_DIM), lambda b, ch, pt: (pt[b, ch * ppc + p], 0, 0))
    in_specs += [page(p) for p in range(ppc)] * 2
    args += [k_pages] * ppc + [v_pages] * ppc
    kern = functools.partial(_decode_kernel, scale=scale, ppc=ppc, n_pages=n_pages, masked=row_mask is not None,
                             diff_lam_init=None if diff is None else diff[2])
    grid_spec = pltpu.PrefetchScalarGridSpec(
        num_scalar_prefetch=1,
        grid=(bs, n_chunks),
        in_specs=in_specs,
        out_specs=pl.BlockSpec((None, r_out, HEAD_DIM), lambda b, ch, pt: (b, 0, 0)),
        scratch_shapes=[pltpu.VMEM((r, 1), F32), pltpu.VMEM((r, 1), F32), pltpu.VMEM((r, HEAD_DIM), F32),
                        pltpu.VMEM((ppc, r, c), F32)],
    )
    return pl.pallas_call(
        kern,
        grid_spec=grid_spec,
        out_shape=jax.ShapeDtypeStruct((bs, r_out, HEAD_DIM), out_dtype),
        compiler_params=_params(2),
        name=name,
    )(page_table, *args)


def _pad_rows(x, n):
    return jnp.pad(x, ((0, 0), (0, n - x.shape[1]), (0, 0)))


def _moba_decode_kernel(*refs, bps):
    pt_ref, q_ref, kn_ref, vn_ref, bnew_ref, tiles_ref = refs[:6]
    k_refs = refs[6:6 + 2 * bps]
    v_refs = refs[6 + 2 * bps:6 + 4 * bps]
    o_ref, ssel_ref, mb_ref, lb_ref, ab_ref, s_ref = refs[6 + 4 * bps:]
    j = pl.program_id(1)
    nb = pl.num_programs(1)
    n_blk = mb_ref.shape[0]
    q = q_ref[...].astype(BF16)
    rows = q.shape[0]
    scale = HEAD_DIM ** -0.5
    lane = lax.broadcasted_iota(jnp.int32, (rows, LANES), 1)

    @pl.when(j == 0)
    def _init():
        ssel_ref[...] = jnp.full_like(ssel_ref, -jnp.inf)

    for i in range(bps):
        blk = j * bps + i
        k0_ref, k1_ref, v0_ref, v1_ref = k_refs[2 * i], k_refs[2 * i + 1], v_refs[2 * i], v_refs[2 * i + 1]
        ksum = (jnp.sum(k0_ref[...].reshape(PAGE_SIZE, MOBA_HEADS, HEAD_DIM), axis=0)
                + jnp.sum(k1_ref[...].reshape(PAGE_SIZE, MOBA_HEADS, HEAD_DIM), axis=0))
        last = tiles_ref[0]
        if i == bps - 1:
            last = jnp.where(j == nb - 1, tiles_ref[1], last)
        blocks = [(lambda r=k0_ref: r[...].astype(BF16), lambda r=v0_ref: r[...].astype(BF16),
                   lambda s: s * scale + tiles_ref[0]),
                  (lambda r=k1_ref: r[...].astype(BF16), lambda r=v1_ref: r[...].astype(BF16),
                   lambda s, last=last: s * scale + last)]
        m_blk, l_blk, acc_blk = _attend_blocks(q, blocks, s_ref.at[i])
        mb_ref[blk] = m_blk
        lb_ref[blk] = l_blk
        ab_ref[blk] = acc_blk
        kmean = (ksum * (1.0 / MOBA_BLOCK)).astype(BF16).astype(F32)
        kmean_rows = jnp.concatenate([kmean] * (rows // MOBA_HEADS), axis=0)
        score = jnp.sum(q.astype(F32) * kmean_rows, -1, keepdims=True)
        ssel_ref[...] = jnp.where(lane == blk, score, ssel_ref[...])

    @pl.when(j == nb - 1)
    def _finish():
        chosen = (lane < n_blk) & _rank_select(ssel_ref[...], lane, n_blk, MOBA_TOPK)
        sel = chosen.astype(F32)
        s_new = _dot_nt(q, kn_ref[...].astype(BF16)) * scale + bnew_ref[...]
        keep = [jnp.sum(jnp.where(lane == b, sel, 0.0), -1, keepdims=True) > 0.5 for b in range(n_blk)]
        m_tot = jnp.max(s_new, -1, keepdims=True)
        for b in range(n_blk):
            m_tot = jnp.maximum(m_tot, jnp.where(keep[b], mb_ref[b], NEG))
        p_new = jnp.exp(s_new - m_tot)
        l_tot = jnp.sum(p_new, -1, keepdims=True)
        acc = _dot(p_new.astype(BF16), vn_ref[...].astype(BF16))
        for b in range(n_blk):
            w = jnp.where(keep[b], jnp.exp(mb_ref[b] - m_tot), 0.0)
            l_tot = l_tot + w * lb_ref[b]
            acc = acc + w * ab_ref[b]
        o_ref[...] = (acc / l_tot).astype(o_ref.dtype)


def moba_decode(page_table, q, k_new, v_new, bias_new, tiles, k_pages, v_pages):
    bs, r, _ = q.shape
    n_pages = page_table.shape[1]
    ppb = MOBA_BLOCK // PAGE_SIZE
    assert ppb == 2 and n_pages % ppb == 0 and n_pages // ppb >= MOBA_TOPK
    n_blk = n_pages // ppb
    bps = math.gcd(n_blk, 4)
    c = k_pages.shape[1]
    per_b = lambda shape: pl.BlockSpec((None,) + shape, lambda b, j, pt: (b, 0, 0))
    const = lambda shape: pl.BlockSpec(shape, lambda b, j, pt: (0,) * len(shape))
    page = lambda p: pl.BlockSpec((None, c, HEAD_DIM), lambda b, j, pt: (pt[b, ppb * bps * j + p], 0, 0))
    pages = [page(p) for p in range(ppb * bps)]
    grid_spec = pltpu.PrefetchScalarGridSpec(
        num_scalar_prefetch=1,
        grid=(bs, n_blk // bps),
        in_specs=[per_b((r, HEAD_DIM)), per_b((LANES, HEAD_DIM)), per_b((LANES, HEAD_DIM)),
                  const((r, LANES)), const((2, r, c))] + pages + pages,
        out_specs=pl.BlockSpec((None, r, HEAD_DIM), lambda b, j, pt: (b, 0, 0)),
        scratch_shapes=[pltpu.VMEM((r, LANES), F32), pltpu.VMEM((n_blk, r, 1), F32), pltpu.VMEM((n_blk, r, 1), F32),
                        pltpu.VMEM((n_blk, r, HEAD_DIM), F32), pltpu.VMEM((bps, ppb, r, c), F32)],
    )
    return pl.pallas_call(
        functools.partial(_moba_decode_kernel, bps=bps),
        grid_spec=grid_spec,
        out_shape=jax.ShapeDtypeStruct((bs, r, HEAD_DIM), BF16),
        compiler_params=_params(2),
        name="moba_decode",
    )(page_table, q, k_new, v_new, bias_new, tiles, *([k_pages] * (ppb * bps)), *([v_pages] * (ppb * bps)))


def _nsa_combine_kernel(oc_ref, os_ref, ow_ref, g_ref, o_ref):
    gate = jax.nn.sigmoid(g_ref[...])
    o = gate[..., 0:1] * oc_ref[...] + gate[..., 1:2] * os_ref[...] + gate[..., 2:3] * ow_ref[...]
    o_ref[...] = o.astype(o_ref.dtype)


def nsa_combine(o_cmp, o_slc, o_win, gates):
    full = lambda a: pl.BlockSpec(a.shape, lambda: (0,) * a.ndim)
    return pl.pallas_call(
        _nsa_combine_kernel,
        in_specs=[full(o_cmp), full(o_slc), full(o_win), full(gates)],
        out_specs=full(o_cmp),
        out_shape=jax.ShapeDtypeStruct(o_cmp.shape, BF16),
        compiler_params=pltpu.CompilerParams(vmem_limit_bytes=VMEM_LIMIT),
        name="nsa_combine",
    )(o_cmp, o_slc, o_win, gates)


AB_MAIN = 2 * DIFF_HEADS * 2 * DIFF_DK + DIFF_HEADS * HEAD_DIM + NSA_HEADS * HEAD_DIM + 6 * NSA_KV_HEADS * HEAD_DIM
ROW_TILE = 512


def _in_proj(h, w, tn):
    m = h.shape[0]
    n = w.shape[1]
    n_main = n // tn * tn
    tm = min(m, 2 * ROW_TILE)
    main = matmul(h, w, n_main, tm, tn, "in_proj")
    if n_main == n:
        return main, None
    tail = jnp.pad(w[:, n_main:], ((0, 0), (0, LANES - (n - n_main))))
    return main, matmul(h, tail, LANES, tm, LANES, "in_proj_tail")[:, :n - n_main]


def _in_proj_split(h, w, widths, lead):
    m = h.shape[0]
    tm = min(m, 2 * ROW_TILE)
    outs, col = [], 0
    for width in widths:
        tn = math.gcd(width, 1024)
        outs.append(matmul(h, w, width, tm, tn, "in_proj", col0=col).reshape(*lead, width))
        col += width
    return outs


def _even_prompt(hp, w_in, b, s, lam, subln, lam_init, pe, w1, w2, tab, tiles, far):
    hd = DIFF_HEADS * HEAD_DIM
    kv = NSA_KV_HEADS * HEAD_DIM
    dq, dk, dv, nq, kc, vc, ks, vs, kw, vw = _in_proj_split(hp, w_in, (hd,) * 4 + (kv,) * 6, (b, s))
    tail = jnp.pad(w_in[:, AB_MAIN:], ((0, 0), (0, LANES - (w_in.shape[1] - AB_MAIN))))
    gates = matmul(hp, tail, LANES, min(b * s, 2 * ROW_TILE), LANES, "in_proj_tail")[:, :w_in.shape[1] - AB_MAIN]
    o_diff = diff_prompt(dq, dk, dv, tiles[:DIFF_HEADS], far[:DIFF_HEADS], lam, subln, lam_init)
    tab_n = tab[:, DIFF_HEADS:]
    kcmp = nsa_compress(kc, pe[0], w1[0], w2[0], None)
    vcmp = nsa_compress(vc, pe[1], w1[1], w2[1], None)
    o_cmp, sel = nsa_cmp_select(nq, 0, ATT_TILE, kcmp, vcmp, tab_n, 0, s)
    o_nsa = nsa_prompt(nq, ks, vs, kw, vw, sel, o_cmp, _gate_pad(gates).reshape(b, s, NSA_KV_HEADS * LANES),
                       tiles[DIFF_HEADS:], far[DIFF_HEADS:])
    keep = min(NSA_WINDOW, s)
    state = (dk.reshape(b, s, DIFF_HEADS, HEAD_DIM), dv.reshape(b, s, DIFF_HEADS, HEAD_DIM),
             *[a.reshape(b, s, NSA_KV_HEADS, HEAD_DIM) for a in (kc, vc, ks, vs)],
             kw[:, s - keep:].reshape(b, keep, NSA_KV_HEADS, HEAD_DIM),
             vw[:, s - keep:].reshape(b, keep, NSA_KV_HEADS, HEAD_DIM))
    return o_diff.reshape(b * s, hd), o_nsa.reshape(b * s, NSA_HEADS * HEAD_DIM), state


def _even_sample(proj, gates, bs, ts, page_table, cache_dk, cache_dv, cache_ck, cache_cv, cache_sk, cache_sv,
                 win_k, win_v, lam, subln, lam_init, pe, w1, w2, tab):
    n_pool = cache_dk.shape[0]
    n_pages = page_table.shape[1]
    past = n_pages * PAGE_SIZE
    hd = DIFF_HEADS * HEAD_DIM
    kv = NSA_KV_HEADS * HEAD_DIM
    p3 = proj.reshape(bs, ts, AB_MAIN)
    dq, dk, dv, nq = [p3[..., i * hd:(i + 1) * hd] for i in range(4)]
    c0 = AB_MAIN - 6 * kv
    kc, vc, ks, vs, kw, vw = [p3[..., c0 + i * kv:c0 + (i + 1) * kv] for i in range(6)]
    tab_d, tab_n = tab[:, :DIFF_HEADS], tab[:, DIFF_HEADS:]
    new_pos = np.where(np.arange(LANES) < ts, past + np.arange(LANES), -1)

    r = ts * DIFF_HEADS
    q = dq.reshape(bs, r, HEAD_DIM)
    lane = jnp.arange(HEAD_DIM)
    q2 = jnp.concatenate([jnp.where(lane < DIFF_DK, q, 0.0), jnp.where(lane >= DIFF_DK, q, 0.0)], axis=1)
    row_t = np.tile(np.repeat(np.arange(ts), DIFF_HEADS), 2)
    row_h = np.tile(np.arange(DIFF_HEADS), 2 * ts)
    page_pos = lambda p: p * PAGE_SIZE + np.arange(PAGE_SIZE)
    d_bias = lambda kpos: _decode_bias(tab_d, row_t, row_h, row_h, DIFF_HEADS, past, kpos)
    tiles_d = jnp.stack([d_bias(page_pos(p)) for p in (0, 1, n_pages - 1)])
    o_diff = decode_attn(page_table, q2, _pad_rows(dk.reshape(bs, r, HEAD_DIM), LANES),
                         _pad_rows(dv.reshape(bs, r, HEAD_DIM), LANES), d_bias(new_pos[:LANES // DIFF_HEADS]), tiles_d,
                         cache_dk.reshape(n_pool, PAGE_SIZE * DIFF_HEADS, HEAD_DIM),
                         cache_dv.reshape(n_pool, PAGE_SIZE * DIFF_HEADS, HEAD_DIM),
                         1, DIFF_DK ** -0.5, BF16, diff=(lam, subln, lam_init), name="diff_decode")

    rn = ts * NSA_HEADS
    qn = nq.reshape(bs, rn, HEAD_DIM)
    row_tn = np.repeat(np.arange(ts), NSA_HEADS)
    row_cn = np.tile(np.arange(NSA_HEADS), ts)
    row_gn = row_cn // NSA_GROUP
    kcmp = nsa_compress(cache_ck, pe[0], w1[0], w2[0], page_table)
    vcmp = nsa_compress(cache_cv, pe[1], w1[1], w2[1], page_table)
    o_cmp, sel = nsa_cmp_select(p3, 6, ts, kcmp, vcmp, tab_n, past, past + ts)
    row_mask = jnp.broadcast_to(jnp.swapaxes(sel, 1, 2)[:, :, :, None, :], (bs, ts, NSA_KV_HEADS, NSA_GROUP, LANES))
    n_bias = lambda kpos, window=None: _decode_bias(tab_n, row_tn, row_cn, row_gn, NSA_KV_HEADS, past, kpos, window)
    new_n = new_pos[:LANES // NSA_KV_HEADS]
    k_new = lambda a: _pad_rows(a.reshape(bs, ts * NSA_KV_HEADS, HEAD_DIM), LANES)
    tiles_s = jnp.stack([n_bias(page_pos(p)) for p in (0, 1, n_pages - 1)])
    o_slc = decode_attn(page_table, qn, k_new(ks), k_new(vs), n_bias(new_n), tiles_s,
                        cache_sk.reshape(n_pool, PAGE_SIZE * NSA_KV_HEADS, HEAD_DIM),
                        cache_sv.reshape(n_pool, PAGE_SIZE * NSA_KV_HEADS, HEAD_DIM),
                        1, HEAD_DIM ** -0.5, F32, row_mask=row_mask.reshape(bs, rn, LANES), name="nsa_slc_decode")
    w_buf = win_k.shape[1]
    w_pages = w_buf // PAGE_SIZE
    win_table = jnp.arange(bs * w_pages, dtype=jnp.int32).reshape(bs, w_pages)
    win_pos = lambda p: past - w_buf + p * PAGE_SIZE + np.arange(PAGE_SIZE)
    tiles_w = jnp.stack([n_bias(win_pos(p), NSA_WINDOW) for p in (0, 1, w_pages - 1)])
    o_win = decode_attn(win_table, qn, k_new(kw), k_new(vw), n_bias(new_n, NSA_WINDOW), tiles_w,
                        win_k.reshape(bs * w_pages, PAGE_SIZE * NSA_KV_HEADS, HEAD_DIM),
                        win_v.reshape(bs * w_pages, PAGE_SIZE * NSA_KV_HEADS, HEAD_DIM),
                        1, HEAD_DIM ** -0.5, F32, name="nsa_win_decode")
    o_nsa = nsa_combine(o_cmp.reshape(bs, rn, HEAD_DIM), o_slc, o_win, gates.reshape(bs, rn, 3))

    as_kv = lambda a: a.reshape(bs, ts, NSA_KV_HEADS, HEAD_DIM)
    keep = min(NSA_WINDOW, past + ts)
    state = (dk.reshape(bs, ts, DIFF_HEADS, HEAD_DIM), dv.reshape(bs, ts, DIFF_HEADS, HEAD_DIM),
             as_kv(kc), as_kv(vc), as_kv(ks), as_kv(vs),
             jnp.concatenate([win_k, as_kv(kw)], axis=1)[:, w_buf + ts - keep:],
             jnp.concatenate([win_v, as_kv(vw)], axis=1)[:, w_buf + ts - keep:])
    return o_diff.reshape(bs * ts, hd), o_nsa.reshape(bs * ts, NSA_HEADS * HEAD_DIM), state


def _odd_sample(proj, bs, ts, page_table, cache_k, cache_v, tab):
    n_pool = cache_k.shape[0]
    n_pages = page_table.shape[1]
    past = n_pages * PAGE_SIZE
    hd = MOBA_HEADS * HEAD_DIM
    p3 = proj.reshape(bs, ts, 3 * hd)
    q, k, v = [p3[..., i * hd:(i + 1) * hd] for i in range(3)]
    r = ts * MOBA_HEADS
    row_t = np.repeat(np.arange(ts), MOBA_HEADS)
    row_h = np.tile(np.arange(MOBA_HEADS), ts)
    bias = lambda kpos: _decode_bias(tab, row_t, row_h, row_h, MOBA_HEADS, past, kpos)
    page_pos = lambda p: p * PAGE_SIZE + np.arange(PAGE_SIZE)
    new_pos = np.where(np.arange(LANES // MOBA_HEADS) < ts, past + np.arange(LANES // MOBA_HEADS), -1)
    tiles = jnp.stack([bias(page_pos(1)), bias(page_pos(n_pages - 1))])
    o = moba_decode(page_table, q.reshape(bs, r, HEAD_DIM), _pad_rows(k.reshape(bs, r, HEAD_DIM), LANES),
                    _pad_rows(v.reshape(bs, r, HEAD_DIM), LANES), bias(new_pos), tiles,
                    cache_k.reshape(n_pool, PAGE_SIZE * MOBA_HEADS, HEAD_DIM),
                    cache_v.reshape(n_pool, PAGE_SIZE * MOBA_HEADS, HEAD_DIM))
    state = (k.reshape(bs, ts, MOBA_HEADS, HEAD_DIM), v.reshape(bs, ts, MOBA_HEADS, HEAD_DIM))
    return o.reshape(bs * ts, hd), state


def kernel(x_prompt, x_sample, c_prompt, c_sample, page_table, cache_diff_k, cache_diff_v, cache_nsa_cmp_k, cache_nsa_cmp_v, cache_nsa_slc_k, cache_nsa_slc_v, state_nsa_win_k, state_nsa_win_v, cache_moba_k, cache_moba_v, w_in_ab, w_out_ab, diff_lambda, diff_subln_g, nsa_cmp_pe, nsa_cmp_w1, nsa_cmp_w2, w_in_c, w_out_c, rel_bias, ada_w, ada_b, ln_g, ln_b, router_group_w, router_group_b, router_expert_w, router_expert_b, moe_w_gate, moe_w_up, moe_w_down):
    b, s, d = x_prompt.shape
    bs, ts, _ = x_sample.shape
    mp, ms = b * s, bs * ts
    depth = ada_w.shape[0]
    mod = ada_mod(jnp.concatenate([c_prompt, c_sample]), ada_w.reshape(2 * depth, d, 3 * d), ada_b.reshape(2 * depth, 3 * d))

    def mod_vectors(sub):
        per_tok = jnp.repeat(mod[sub, b:], ts, axis=0)
        return ([mod[sub, :b, None, i * d:(i + 1) * d] for i in range(3)],
                [per_tok[None, :, i * d:(i + 1) * d] for i in range(3)])

    xp, xs = x_prompt.reshape(mp, d), x_sample.reshape(ms, d)
    tiles, far = _toeplitz_tiles(rel_bias, ATT_TILE), _far_bias(rel_bias)
    even_p, even_s, odd_p, odd_s = [], [], [], []
    for l in range(depth):
        i = l // 2
        (shp, scp, gp), (shs, scs, gs) = mod_vectors(2 * l)
        hp = mod_cast(xp, shp, scp, ROW_TILE)
        hs = mod_cast(xs, shs, scs, ms)
        if l % 2 == 0:
            lam_init = 0.8 - 0.6 * math.exp(-0.3 * l)
            w1, w2 = nsa_cmp_w1[i].astype(BF16), nsa_cmp_w2[i].astype(BF16)
            proj_s, gates_s = _in_proj(hs, w_in_ab[i], 512)
            oa_p, ob_p, st_p = _even_prompt(hp, w_in_ab[i], b, s, diff_lambda[i], diff_subln_g[i], lam_init,
                                            nsa_cmp_pe[i], w1, w2, rel_bias, tiles, far)
            oa_s, ob_s, st_s = _even_sample(proj_s, gates_s, bs, ts, page_table, cache_diff_k[i], cache_diff_v[i],
                                            cache_nsa_cmp_k[i], cache_nsa_cmp_v[i], cache_nsa_slc_k[i],
                                            cache_nsa_slc_v[i], state_nsa_win_k[i], state_nsa_win_v[i],
                                            diff_lambda[i], diff_subln_g[i], lam_init, nsa_cmp_pe[i], w1, w2, rel_bias)
            even_p.append(st_p)
            even_s.append(st_s)
            w_out = w_out_ab[i].astype(BF16)
            cols = (0, 0)
        else:
            proj_s, _ = _in_proj(hs, w_in_c[i], 1024)
            hd = MOBA_HEADS * HEAD_DIM
            q_p, k_p, v_p = _in_proj_split(hp, w_in_c[i], (hd,) * 3, (b, s))
            oa_p = ob_p = moba_prompt(q_p, k_p, v_p, tiles, far).reshape(mp, hd)
            odd_p.append((k_p.reshape(b, s, MOBA_HEADS, HEAD_DIM), v_p.reshape(b, s, MOBA_HEADS, HEAD_DIM)))
            oa_s, st_s = _odd_sample(proj_s, bs, ts, page_table, cache_moba_k[i], cache_moba_v[i], rel_bias)
            ob_s = oa_s
            odd_s.append(st_s)
            w_out = w_out_c[i].astype(BF16)
            cols = (0, 1)
        xp = out_proj_norm(oa_p, ob_p, cols[0], cols[1], w_out, xp, gp, ln_g[l, 0], ln_b[l, 0], 256)
        xs = out_proj_norm(oa_s, ob_s, cols[0], cols[1], w_out, xs, gs, ln_g[l, 0], ln_b[l, 0], ms)

        (shp, scp, gp), (shs, scs, gs) = mod_vectors(2 * l + 1)
        hp = mod_cast(xp, shp, scp, ROW_TILE)
        hs = mod_cast(xs, shs, scs, ms)
        rw, rb = _router_pack(router_group_w[l], router_group_b[l], router_expert_w[l], router_expert_b[l])
        wg, wu, wd = cast_bf16(moe_w_gate, l), cast_bf16(moe_w_up, l), cast_bf16(moe_w_down, l)
        xp = moe_norm(hp, rw, rb, wg, wu, wd, xp, gp, ln_g[l, 1], ln_b[l, 1], ROW_TILE)
        xs = moe_norm(hs, rw, rb, wg, wu, wd, xs, gs, ln_g[l, 1], ln_b[l, 1], ms)

    stack = lambda rows, j: jnp.stack([r[j] for r in rows])
    outs = [xp.reshape(b, s, d), xs.reshape(bs, ts, d)]
    for j in range(8):
        outs += [stack(even_p, j), stack(even_s, j)]
    for j in range(2):
        outs += [stack(odd_p, j), stack(odd_s, j)]
    return tuple(outs)
```
